```python
import jax, jax.numpy as jnp
from jax import lax
import numpy as np

D_MODEL = 2048
BATCH = 16
SEQ = 2048
DEPTH = 4
DEC_BATCH = 32
DEC_SEQ = 16
PAST_LEN = 1024

CHUNK = 64
Q_BLOCK = 128
SB_HEADS = 8
SB_HEAD_DIM = 128
SB_WIDTH = SB_HEADS * SB_HEAD_DIM
SGU_GROUPS = 8
SGU_CHUNK = 128
SGU_WIDTH = D_MODEL // 2
SGU_GROUP_DIM = SGU_WIDTH // SGU_GROUPS
GLA_HEADS = 4
GLA_DK = (D_MODEL // 2) // GLA_HEADS
GLA_DV = D_MODEL // GLA_HEADS
GLA_GATE_RANK = 16
GLA_TAU = 16.0
D_FF = 4 * D_MODEL
N_EVEN = (DEPTH + 1) // 2
N_ODD = DEPTH // 2
EVEN_IN = 3 * SB_WIDTH + 2 * SGU_WIDTH
GLA_QK = GLA_HEADS * GLA_DK
GLA_VW = GLA_HEADS * GLA_DV
ODD_IN = 2 * GLA_QK + 2 * GLA_VW + GLA_GATE_RANK
EPS = 1e-6

kernel_name = 'streaming_hybrid_sb_sgu_gla'


def rmsnorm(x, g):
    xf = x.astype(jnp.float32)
    y = xf * lax.rsqrt(jnp.mean(xf * xf, axis=-1, keepdims=True) + EPS)
    return (y * g.astype(jnp.float32)).astype(x.dtype)


def modulate(x, shift, scale):
    return x * (1 + scale[:, None, :]) + shift[:, None, :]


def stick_breaking(q, k, v, q_pos, k_pos):
    z = jnp.einsum('bqhd,bkhd->bhqk', q.astype(jnp.float32), k.astype(jnp.float32)) * (SB_HEAD_DIM ** -0.5)
    mask = k_pos[None, :] < q_pos[:, None]
    neg = jnp.where(mask, jax.nn.log_sigmoid(-z), 0.0)
    suffix = lax.cumsum(neg, axis=3, reverse=True) - neg
    w = jnp.where(mask, jnp.exp(jax.nn.log_sigmoid(z) + suffix), 0.0)
    return jnp.einsum('bhqk,bkhd->bqhd', w, v.astype(jnp.float32)).astype(v.dtype)


def sb_prompt(q, k, v):
    B, L = q.shape[0], q.shape[1]
    nb = L // Q_BLOCK
    pos = jnp.arange(L, dtype=jnp.int32)
    qb = q.reshape(B, nb, Q_BLOCK, SB_HEADS, SB_HEAD_DIM).swapaxes(0, 1)
    out = lax.map(lambda a: stick_breaking(a[0], k, v, a[1], pos), (qb, pos.reshape(nb, Q_BLOCK)))
    return out.swapaxes(0, 1).reshape(B, L, SB_HEADS, SB_HEAD_DIM)


def gla_chunk(S0, q, k, v, lg):
    L = q.shape[1]
    qf, kf, vf = q.astype(jnp.float32), k.astype(jnp.float32), v.astype(jnp.float32)
    b = jnp.cumsum(lg, axis=1)
    causal = jnp.tril(jnp.ones((L, L), dtype=bool))[None, :, :, None, None]
    decay = jnp.exp(jnp.where(causal, b[:, :, None] - b[:, None, :], -jnp.inf))
    scores = jnp.einsum('bthd,bshd,btshd->bhts', qf, kf, decay)
    o = jnp.einsum('bthd,bhde->bthe', qf * jnp.exp(b), S0) + jnp.einsum('bhts,bshe->bthe', scores, vf)
    b_last = b[:, -1]
    S_new = jnp.exp(b_last)[..., None] * S0 + jnp.einsum('bshd,bshe->bhde', kf * jnp.exp(b_last[:, None] - b), vf)
    return S_new, o.astype(v.dtype)


def even_mixer(h, w_in, w_out, g_sgu, w_sgu, b_sgu, past_k, past_v):
    B, L, _ = h.shape
    q, k, v, u, gv = jnp.split(h @ w_in, [SB_WIDTH, 2 * SB_WIDTH, 3 * SB_WIDTH, 3 * SB_WIDTH + SGU_WIDTH], axis=-1)
    q = q.reshape(B, L, SB_HEADS, SB_HEAD_DIM)
    k = k.reshape(B, L, SB_HEADS, SB_HEAD_DIM)
    v = v.reshape(B, L, SB_HEADS, SB_HEAD_DIM)
    if past_k is None:
        o_a = sb_prompt(q, k, v)
        chunk_len = SGU_CHUNK
    else:
        P = past_k.shape[1]
        kk = jnp.concatenate([past_k.astype(k.dtype), k], axis=1)
        vv = jnp.concatenate([past_v.astype(v.dtype), v], axis=1)
        o_a = stick_breaking(q, kk, vv, P + jnp.arange(L, dtype=jnp.int32), jnp.arange(P + L, dtype=jnp.int32))
        chunk_len = L
    gv = rmsnorm(jax.nn.gelu(gv), g_sgu)
    vg = gv.reshape(B, L // chunk_len, chunk_len, SGU_GROUPS, SGU_GROUP_DIM)
    tril = jnp.tril(jnp.ones((SGU_CHUNK, SGU_CHUNK), dtype=bool))
    w = jnp.where(tril, w_sgu, 0.0)[:, :chunk_len, :chunk_len]
    mix = jnp.einsum('gts,bnsgc->bntgc', w, vg) + b_sgu[:, :chunk_len].T[:, :, None]
    o_b = jax.nn.gelu(u) * mix.reshape(B, L, SGU_WIDTH)
    y = jnp.concatenate([o_a.reshape(B, L, SB_WIDTH), o_b], axis=-1) @ w_out
    return y, k, v, gv


def odd_mixer(h, w_in, w_gate_up, b_gate, g_gla, w_out, past_state):
    B, L, _ = h.shape
    q, k, v, r, a = jnp.split(h @ w_in, [GLA_QK, 2 * GLA_QK, 2 * GLA_QK + GLA_VW, 2 * GLA_QK + 2 * GLA_VW], axis=-1)
    lg = jax.nn.log_sigmoid((a @ w_gate_up + b_gate).astype(jnp.float32)) / GLA_TAU
    q = (q * (GLA_DK ** -0.5)).reshape(B, L, GLA_HEADS, GLA_DK)
    k = k.reshape(B, L, GLA_HEADS, GLA_DK)
    lg = lg.reshape(B, L, GLA_HEADS, GLA_DK)
    v = v.reshape(B, L, GLA_HEADS, GLA_DV)
    r = r.reshape(B, L, GLA_HEADS, GLA_DV)
    if past_state is None:
        n = L // CHUNK
        blocks = lambda t: t.reshape(B, n, CHUNK, *t.shape[2:]).swapaxes(0, 1)
        S0 = jnp.zeros((B, GLA_HEADS, GLA_DK, GLA_DV), jnp.float32)
        S, o = lax.scan(lambda S_, xs: gla_chunk(S_, *xs), S0, (blocks(q), blocks(k), blocks(v), blocks(lg)))
        o = o.swapaxes(0, 1).reshape(B, L, GLA_HEADS, GLA_DV)
    else:
        S, o = gla_chunk(past_state.astype(jnp.float32), q, k, v, lg)
    o = rmsnorm(o, g_gla) * jax.nn.silu(r)
    return o.reshape(B, L, GLA_VW) @ w_out, S


def channel_mlp(h, w_up, w_down):
    return jnp.square(jax.nn.relu(h @ w_up)) @ w_down


def trunk(x, c, past_k, past_v, past_state, w_mod, b_mod, g_norm, w_in_even, w_out_even, g_sgu, w_sgu, b_sgu,
          w_in_odd, w_gate_up, b_gate, g_gla, w_out_odd, w_ff_up, w_ff_down):
    sample = past_k is not None
    cs = jax.nn.silu(c)
    ks, vs, gvs, states = [], [], [], []
    for l in range(DEPTH):
        i = l // 2
        sh_m, sc_m, gt_m, sh_f, sc_f, gt_f = jnp.split(cs @ w_mod[l] + b_mod[l], 6, axis=-1)
        h = modulate(rmsnorm(x, g_norm[l, 0]), sh_m, sc_m)
        if l % 2 == 0:
            y, k, v, gv = even_mixer(h, w_in_even[i], w_out_even[i], g_sgu[i], w_sgu[i], b_sgu[i],
                                     past_k[i] if sample else None, past_v[i] if sample else None)
            ks.append(k)
            vs.append(v)
            gvs.append(gv)
        else:
            y, S = odd_mixer(h, w_in_odd[i], w_gate_up[i], b_gate[i], g_gla[i], w_out_odd[i],
                             past_state[i] if sample else None)
            states.append(S)
        x = x + gt_m[:, None, :] * rmsnorm(y, g_norm[l, 1])
        h = modulate(rmsnorm(x, g_norm[l, 2]), sh_f, sc_f)
        x = x + gt_f[:, None, :] * rmsnorm(channel_mlp(h, w_ff_up[l], w_ff_down[l]), g_norm[l, 3])
    return x, jnp.stack(ks), jnp.stack(vs), jnp.stack(states), jnp.stack(gvs)


def setup_inputs(seed: int = 0) -> dict:
    key = jax.random.key(seed)
    ks = jax.random.split(key, 24)
    nrm = lambda k, shape, s=1.0: jax.random.normal(k, shape, jnp.float32) * s
    return {
        'x_prompt': nrm(ks[0], (BATCH, SEQ, D_MODEL)),
        'x_sample': nrm(ks[1], (DEC_BATCH, DEC_SEQ, D_MODEL)),
        'cache_sb_k': nrm(ks[2], (N_EVEN, DEC_BATCH, PAST_LEN, SB_HEADS, SB_HEAD_DIM)),
        'cache_sb_v': nrm(ks[3], (N_EVEN, DEC_BATCH, PAST_LEN, SB_HEADS, SB_HEAD_DIM)),
        'state_gla': nrm(ks[4], (N_ODD, DEC_BATCH, GLA_HEADS, GLA_DK, GLA_DV)),
        'c_prompt': nrm(ks[5], (BATCH, D_MODEL)),
        'c_sample': nrm(ks[6], (DEC_BATCH, D_MODEL)),
        'w_mod': nrm(ks[7], (DEPTH, D_MODEL, 6 * D_MODEL), 0.5 * D_MODEL ** -0.5),
        'b_mod': nrm(ks[8], (DEPTH, 6 * D_MODEL), 0.02),
        'g_norm': 1.0 + nrm(ks[9], (DEPTH, 4, D_MODEL), 0.02),
        'w_in_even': nrm(ks[10], (N_EVEN, D_MODEL, EVEN_IN), D_MODEL ** -0.5),
        'w_out_even': nrm(ks[11], (N_EVEN, SB_WIDTH + SGU_WIDTH, D_MODEL), (SB_WIDTH + SGU_WIDTH) ** -0.5),
        'g_sgu': 1.0 + nrm(ks[12], (N_EVEN, SGU_WIDTH), 0.02),
        'w_sgu': nrm(ks[13], (N_EVEN, SGU_GROUPS, SGU_CHUNK, SGU_CHUNK), 0.5 * SGU_CHUNK ** -0.5),
        'b_sgu': 1.0 + nrm(ks[14], (N_EVEN, SGU_GROUPS, SGU_CHUNK), 0.1),
        'w_in_odd': nrm(ks[15], (N_ODD, D_MODEL, ODD_IN), D_MODEL ** -0.5),
        'w_gate_up': nrm(ks[16], (N_ODD, GLA_GATE_RANK, GLA_QK), GLA_GATE_RANK ** -0.5),
        'b_gate': nrm(ks[17], (N_ODD, GLA_QK), 0.1),
        'g_gla': 1.0 + nrm(ks[18], (N_ODD, GLA_DV), 0.02),
        'w_out_odd': nrm(ks[19], (N_ODD, GLA_VW, D_MODEL), GLA_VW ** -0.5),
        'w_ff_up': nrm(ks[20], (DEPTH, D_MODEL, D_FF), D_MODEL ** -0.5),
        'w_ff_down': nrm(ks[21], (DEPTH, D_FF, D_MODEL), D_FF ** -0.5),
    }


def reference(x_prompt, x_sample, cache_sb_k, cache_sb_v, state_gla, c_prompt, c_sample, w_mod, b_mod, g_norm,
              w_in_even, w_out_even, g_sgu, w_sgu, b_sgu, w_in_odd, w_gate_up, b_gate, g_gla, w_out_odd,
              w_ff_up, w_ff_down):
    y_prompt, sb_k_prompt, sb_v_prompt, gla_state_prompt, _ = trunk(
        x_prompt, c_prompt, None, None, None, w_mod, b_mod, g_norm, w_in_even, w_out_even, g_sgu, w_sgu, b_sgu,
        w_in_odd, w_gate_up, b_gate, g_gla, w_out_odd, w_ff_up, w_ff_down)
    y_sample, sb_k_sample, sb_v_sample, gla_state_sample, sgu_v_sample = trunk(
        x_sample, c_sample, cache_sb_k, cache_sb_v, state_gla, w_mod, b_mod, g_norm, w_in_even, w_out_even, g_sgu,
        w_sgu, b_sgu, w_in_odd, w_gate_up, b_gate, g_gla, w_out_odd, w_ff_up, w_ff_down)
    return (y_prompt, y_sample, sb_k_prompt, sb_v_prompt, gla_state_prompt,
            sb_k_sample, sb_v_sample, gla_state_sample, sgu_v_sample)
```

```python
import functools

import jax
import jax.numpy as jnp
from jax import lax
from jax.experimental import pallas as pl
from jax.experimental.pallas import tpu as pltpu

F32 = jnp.float32
BF16 = jnp.bfloat16

EPS = 1e-6
GLA_CHUNK = 64
GLA_TAU = 16.0
GLA_DIRECT = 8
LANES = 128
VMEM_LIMIT_BYTES = 56 * 1024 * 1024


def _params(n_grid):
    return pltpu.CompilerParams(
        dimension_semantics=("arbitrary",) * n_grid,
        vmem_limit_bytes=VMEM_LIMIT_BYTES)


def _dot(a, b):
    return jnp.dot(a, b, preferred_element_type=F32)


def _dot_nt(a, b):
    return lax.dot_general(a, b, (((1,), (1,)), ((), ())), preferred_element_type=F32)


def _dot_tn(a, b):
    return lax.dot_general(a, b, (((0,), (0,)), ((), ())), preferred_element_type=F32)


def _split_bf16(x):
    hi = x.astype(BF16)
    lo = (x - hi.astype(F32)).astype(BF16)
    return hi, lo


def _rms(x, g):
    return x * lax.rsqrt(jnp.mean(x * x, axis=-1, keepdims=True) + EPS) * g


def _softplus(z):
    return jnp.maximum(z, 0.0) + jnp.log1p(jnp.exp(-jnp.abs(z)))


def _silu(x):
    return x * jax.nn.sigmoid(x)


def _rows(ref, rows):
    v = ref[...]
    nb, _, d = v.shape
    if nb == 1:
        return v[0]
    return jnp.broadcast_to(v, (nb, rows, d)).reshape(nb * rows, d)


class _Group:
    def __init__(self, n_seq, seq_len, mod_row0, tm_pref):
        self.n_seq, self.seq_len, self.mod_row0 = n_seq, seq_len, mod_row0
        self.n = n_seq * seq_len
        if seq_len >= tm_pref:
            assert seq_len % tm_pref == 0
            self.tm, self.nb = tm_pref, 1
        else:
            self.nb = min(n_seq, tm_pref // seq_len)
            assert n_seq % self.nb == 0
            self.tm = self.nb * seq_len
        self.rows = self.tm // self.nb
        self.blocks_per_seq = max(1, seq_len // self.tm)

    def mod_spec(self, layer, col, d, n_grid):
        nb, bps, row0 = self.nb, self.blocks_per_seq, self.mod_row0

        def index(i, *_):
            if nb == 1:
                return (layer, row0 + i // bps, 0, col)
            return (layer, row0 // nb + i, 0, col)

        return pl.BlockSpec((None, nb, 1, d), index)


def _mod_kernel(c_ref, w_ref, b_ref, o_ref):
    cs = _silu(c_ref[...]).astype(BF16)
    o_ref[...] = _dot(cs, w_ref[...].astype(BF16)) + b_ref[...]


def _modulation(c_all, w_mod, b_mod):
    depth, d, n_out = w_mod.shape
    n_c = c_all.shape[0]
    tn = min(n_out, 1024)
    return pl.pallas_call(
        _mod_kernel,
        out_shape=jax.ShapeDtypeStruct((depth, n_c, n_out), F32),
        grid=(depth, n_out // tn),
        in_specs=[
            pl.BlockSpec((n_c, d), lambda l, j: (0, 0)),
            pl.BlockSpec((None, d, tn), lambda l, j: (l, 0, j)),
            pl.BlockSpec((None, 1, tn), lambda l, j: (l, 0, j)),
        ],
        out_specs=pl.BlockSpec((None, n_c, tn), lambda l, j: (l, 0, j)),
        compiler_params=_params(2),
        name="modulation",
    )(c_all, w_mod, b_mod.reshape(depth, 1, n_out))


def _inproj_kernel(*refs, seg_blocks, rows, has_aux):
    x_ref, g_ref, sh_ref, sc_ref, w_ref = refs[:5]
    pos = 5
    if has_aux:
        wah_ref, wal_ref = refs[5:7]
        pos = 7
    n_seg = len(seg_blocks)
    out_refs = refs[pos:pos + n_seg]
    pos += n_seg
    if has_aux:
        aux_ref = refs[pos]
        pos += 1
    h_ref = refs[pos]
    j = pl.program_id(1)

    @pl.when(j == 0)
    def _():
        h = _rms(x_ref[...], g_ref[...]) * (1.0 + _rows(sc_ref, rows)) + _rows(sh_ref, rows)
        h_hi, h_lo = _split_bf16(h)
        h_ref[...] = h_hi
        if has_aux:
            aux_ref[...] = (_dot(h_hi, wah_ref[...]) + _dot(h_lo, wah_ref[...])
                            + _dot(h_hi, wal_ref[...]))

    y = _dot(h_ref[...], w_ref[...])
    j0 = 0
    for o_ref, nblk in zip(out_refs, seg_blocks):
        @pl.when((j >= j0) & (j < j0 + nblk))
        def _(o_ref=o_ref):
            o_ref[...] = y.astype(o_ref.dtype)
        j0 += nblk


def _inproj(x, mod, layer, g, w, segs, grp, aux_w=None, tn_pref=512):
    n, d = x.shape
    tm = grp.tm
    tn = min(tn_pref, min(wd for wd, _ in segs))
    assert all(wd % tn == 0 for wd, _ in segs)
    seg_blocks = tuple(wd // tn for wd, _ in segs)
    n_out = sum(wd for wd, _ in segs)
    assert w.shape == (d, n_out)
    in_specs = [
        pl.BlockSpec((tm, d), lambda i, j: (i, 0)),
        pl.BlockSpec((1, d), lambda i, j: (0, 0)),
        grp.mod_spec(layer, 0, d, 2),
        grp.mod_spec(layer, 1, d, 2),
        pl.BlockSpec((d, tn), lambda i, j: (0, j)),
    ]
    args = [x, g.reshape(1, d), mod, mod, w]
    out_shape, out_specs = [], []
    j0 = 0
    for (wd, dt), nblk in zip(segs, seg_blocks):
        out_shape.append(jax.ShapeDtypeStruct((n, wd), dt))
        out_specs.append(pl.BlockSpec(
            (tm, tn), lambda i, j, j0=j0, nblk=nblk: (i, jnp.clip(j - j0, 0, nblk - 1))))
        j0 += nblk
    has_aux = aux_w is not None
    if has_aux:
        wah, wal = aux_w
        wa = wah.shape[1]
        in_specs += [pl.BlockSpec((d, wa), lambda i, j: (0, 0))] * 2
        args += [wah, wal]
        out_shape.append(jax.ShapeDtypeStruct((n, wa), F32))
        out_specs.append(pl.BlockSpec((tm, wa), lambda i, j: (i, 0)))
    return pl.pallas_call(
        functools.partial(_inproj_kernel, seg_blocks=seg_blocks, rows=grp.rows, has_aux=has_aux),
        out_shape=out_shape,
        grid=(n // tm, n_out // tn),
        in_specs=in_specs,
        out_specs=out_specs,
        scratch_shapes=[pltpu.VMEM((tm, d), BF16)],
        compiler_params=_params(2),
        name="in_proj",
    )(*args)


def _outproj_kernel(*refs, n_a, rows):
    a_refs = refs[:n_a]
    w_refs = refs[n_a:2 * n_a]
    x_ref, g_ref, gt_ref, o_ref = refs[2 * n_a:]
    y = _dot(a_refs[0][...], w_refs[0][...])
    for a_ref, w_ref in zip(a_refs[1:], w_refs[1:]):
        y = y + _dot(a_ref[...], w_ref[...])
    o_ref[...] = x_ref[...] + _rows(gt_ref, rows) * _rms(y, g_ref[...])


def _outproj(acts, ws, x, mod, layer, g, grp, tm_pref=512):
    n, d = x.shape
    tm = grp.tm
    in_specs = [pl.BlockSpec((tm, a.shape[1]), lambda i: (i, 0)) for a in acts]
    in_specs += [pl.BlockSpec(w.shape, lambda i: (0, 0)) for w in ws]
    in_specs += [
        pl.BlockSpec((tm, d), lambda i: (i, 0)),
        pl.BlockSpec((1, d), lambda i: (0, 0)),
        grp.mod_spec(layer, 2, d, 1),
    ]
    return pl.pallas_call(
        functools.partial(_outproj_kernel, n_a=len(acts), rows=grp.rows),
        out_shape=jax.ShapeDtypeStruct((n, d), F32),
        grid=(n // tm,),
        in_specs=in_specs,
        out_specs=pl.BlockSpec((tm, d), lambda i: (i, 0)),
        compiler_params=_params(1),
        name="out_proj",
    )(*acts, *ws, x, g.reshape(1, d), mod)


def _ff_kernel(x_ref, g1_ref, sh_ref, sc_ref, wu_ref, wd_ref, g2_ref, gt_ref, o_ref,
               h_ref, acc_ref, *, rows):
    j = pl.program_id(1)

    @pl.when(j == 0)
    def _():
        h = _rms(x_ref[...], g1_ref[...]) * (1.0 + _rows(sc_ref, rows)) + _rows(sh_ref, rows)
        h_ref[...] = h.astype(BF16)

    t = jnp.square(jnp.maximum(_dot(h_ref[...], wu_ref[...]), 0.0)).astype(BF16)
    p = _dot(t, wd_ref[...])

    @pl.when(j == 0)
    def _():
        acc_ref[...] = p

    @pl.when(j > 0)
    def _():
        acc_ref[...] += p

    @pl.when(j == pl.num_programs(1) - 1)
    def _():
        o_ref[...] = x_ref[...] + _rows(gt_ref, rows) * _rms(acc_ref[...], g2_ref[...])


def _ff(x, mod, layer, g1, g2, w_up, w_down, grp, tf_pref=512):
    n, d = x.shape
    f = w_up.shape[1]
    tm = grp.tm
    tf = min(tf_pref, f)
    assert f % tf == 0
    return pl.pallas_call(
        functools.partial(_ff_kernel, rows=grp.rows),
        out_shape=jax.ShapeDtypeStruct((n, d), F32),
        grid=(n // tm, f // tf),
        in_specs=[
            pl.BlockSpec((tm, d), lambda i, j: (i, 0)),
            pl.BlockSpec((1, d), lambda i, j: (0, 0)),
            grp.mod_spec(layer, 3, d, 2),
            grp.mod_spec(layer, 4, d, 2),
            pl.BlockSpec((d, tf), lambda i, j: (0, j)),
            pl.BlockSpec((tf, d), lambda i, j: (j, 0)),
            pl.BlockSpec((1, d), lambda i, j: (0, 0)),
            grp.mod_spec(layer, 5, d, 2),
        ],
        out_specs=pl.BlockSpec((tm, d), lambda i, j: (i, 0)),
        scratch_shapes=[pltpu.VMEM((tm, d), BF16), pltpu.VMEM((tm, d), F32)],
        compiler_params=_params(2),
        name="channel_mlp",
    )(x, g1.reshape(1, d), mod, mod, w_up, w_down, g2.reshape(1, d), mod)


def _strict_upper(n):
    r = lax.broadcasted_iota(jnp.int32, (n, n), 0)
    c = lax.broadcasted_iota(jnp.int32, (n, n), 1)
    return jnp.where(r > c, 1.0, 0.0).astype(BF16)


def _sb_prompt_kernel(q_ref, k_ref, v_ref, o_ref, kb_ref, vb_ref, *, t, scale):
    qi = pl.program_id(2)

    @pl.when(qi == 0)
    def _():
        kb_ref[...] = k_ref[...].astype(BF16)
        vb_ref[...] = v_ref[...].astype(BF16)

    q = q_ref[...]
    u = _strict_upper(t)
    dh = q.shape[1]

    def block(start, carry, acc, diagonal):
        kb = kb_ref[pl.ds(start, t), :]
        z = _dot_nt(q, kb) * scale
        sp = _softplus(z)
        if diagonal:
            r = lax.broadcasted_iota(jnp.int32, (t, t), 0)
            c = lax.broadcasted_iota(jnp.int32, (t, t), 1)
            mask = c < r
            neg = jnp.where(mask, -sp, 0.0)
        else:
            neg = -sp
        hi, lo = _split_bf16(neg)
        suffix = _dot(hi, u) + _dot(lo, u)
        w = jnp.exp(z - sp + suffix + carry)
        if diagonal:
            w = jnp.where(mask, w, 0.0)
        acc = acc + _dot(w.astype(BF16), vb_ref[pl.ds(start, t), :])
        carry = carry + jnp.sum(neg, axis=-1, keepdims=True)
        return carry, acc

    carry, acc = block(pl.multiple_of(qi * t, t), jnp.zeros((t, 1), F32),
                       jnp.zeros((t, dh), F32), True)

    def body(it, state):
        start = pl.multiple_of((qi - 1 - it) * t, t)
        return block(start, state[0], state[1], False)

    carry, acc = lax.fori_loop(0, qi, body, (carry, acc))
    o_ref[...] = acc.astype(o_ref.dtype)


def _sb_prompt(q, k, v, n_heads, t_pref=256):
    b, l, w = q.shape
    dh = w // n_heads
    t = min(t_pref, l)
    assert l % t == 0
    return pl.pallas_call(
        functools.partial(_sb_prompt_kernel, t=t, scale=dh ** -0.5),
        out_shape=jax.ShapeDtypeStruct((b, l, w), BF16),
        grid=(b, n_heads, l // t),
        in_specs=[
            pl.BlockSpec((None, t, dh), lambda bi, h, qi: (bi, qi, h)),
            pl.BlockSpec((None, l, dh), lambda bi, h, qi: (bi, 0, h)),
            pl.BlockSpec((None, l, dh), lambda bi, h, qi: (bi, 0, h)),
        ],
        out_specs=pl.BlockSpec((None, t, dh), lambda bi, h, qi: (bi, qi, h)),
        scratch_shapes=[pltpu.VMEM((l, dh), BF16), pltpu.VMEM((l, dh), BF16)],
        compiler_params=_params(3),
        name="stick_breaking_prompt",
    )(q, k, v)


def _sb_sample_kernel(q_ref, kn_ref, vn_ref, kc_ref, vc_ref, o_ref, *, n_heads, tk):
    lq, w = q_ref.shape
    dh = w // n_heads
    past = kc_ref.shape[0]
    nl = n_heads * lq
    scale = dh ** -0.5

    qf = q_ref[...].astype(F32)
    row_h = lax.broadcasted_iota(jnp.int32, (nl, w), 0) // lq
    col_h = lax.broadcasted_iota(jnp.int32, (nl, w), 1) // dh
    q_exp = jnp.where(row_h == col_h, jnp.concatenate([qf] * n_heads, axis=0), 0.0).astype(BF16)

    def strict_lower_t(n):
        r = lax.broadcasted_iota(jnp.int32, (n, n), 0)
        c = lax.broadcasted_iota(jnp.int32, (n, n), 1)
        return jnp.where(c > r, 1.0, 0.0).astype(BF16)

    def block(kb, vb, carry, acc, new):
        n = kb.shape[0]
        z = _dot_nt(kb, q_exp) * scale
        sp = _softplus(z)
        if new:
            key = lax.broadcasted_iota(jnp.int32, (n, nl), 0)
            qry = lax.broadcasted_iota(jnp.int32, (n, nl), 1) % lq
            mask = key < qry
            neg = jnp.where(mask, -sp, 0.0)
        else:
            neg = -sp
        hi, lo = _split_bf16(neg)
        ut = strict_lower_t(n)
        suffix = _dot(ut, hi) + _dot(ut, lo)
        wgt = jnp.exp(z - sp + suffix + carry)
        if new:
            wgt = jnp.where(mask, wgt, 0.0)
        acc = acc + _dot(wgt.T.astype(BF16), vb)
        carry = carry + jnp.sum(neg, axis=0, keepdims=True)
        return carry, acc

    pad = jnp.zeros((LANES - lq, w), F32)
    kn = jnp.concatenate([kn_ref[...], pad], axis=0).astype(BF16)
    vn = jnp.concatenate([vn_ref[...], pad], axis=0).astype(BF16)
    carry, acc = block(kn, vn, jnp.zeros((1, nl), F32), jnp.zeros((nl, w), F32), True)
    for i in reversed(range(past // tk)):
        kb = kc_ref[i * tk:(i + 1) * tk, :].astype(BF16)
        vb = vc_ref[i * tk:(i + 1) * tk, :].astype(BF16)
        carry, acc = block(kb, vb, carry, acc, False)

    out = jnp.concatenate(
        [acc[h * lq:(h + 1) * lq, h * dh:(h + 1) * dh] for h in range(n_heads)], axis=1)
    o_ref[...] = out.astype(o_ref.dtype)


def _sb_sample(q, k_new, v_new, cache_k, cache_v, layer_i, n_heads, tk_pref=256):
    b, lq, w = q.shape
    past = cache_k.shape[2]
    tk = min(tk_pref, past)
    assert past % tk == 0 and lq <= LANES
    new_spec = pl.BlockSpec((None, lq, w), lambda bi: (bi, 0, 0))
    cache_spec = pl.BlockSpec((None, None, past, w), lambda bi: (layer_i, bi, 0, 0))
    return pl.pallas_call(
        functools.partial(_sb_sample_kernel, n_heads=n_heads, tk=tk),
        out_shape=jax.ShapeDtypeStruct((b, lq, w), BF16),
        grid=(b,),
        in_specs=[new_spec, new_spec, new_spec, cache_spec, cache_spec],
        out_specs=new_spec,
        compiler_params=_params(1),
        name="stick_breaking_sample",
    )(q, k_new, v_new, cache_k, cache_v)


def _sgu_kernel(u_ref, gv_ref, g_ref, w_ref, b_ref, o_ref, *maybe_vn_ref, r, gd):
    rb, width = u_ref.shape
    vn = _rms(jax.nn.gelu(gv_ref[...]), g_ref[...])
    if maybe_vn_ref:
        maybe_vn_ref[0][...] = vn
    vb = vn.astype(BF16)
    for s in range(rb // r):
        rs = slice(s * r, (s + 1) * r)
        mix = jnp.concatenate(
            [_dot(w_ref[gi], vb[rs, gi * gd:(gi + 1) * gd]) for gi in range(width // gd)], axis=1)
        gate = jax.nn.gelu(u_ref[rs, :].astype(F32))
        o_ref[rs, :] = (gate * (mix + b_ref[...])).astype(o_ref.dtype)


def _sgu(u, gv, g_sgu, w_mix, bias, emit_vn, rb_pref=512):
    n, width = u.shape
    groups, r, _ = w_mix.shape
    rb = max(r, min(rb_pref, n))
    assert n % rb == 0 and rb % r == 0
    row_spec = pl.BlockSpec((rb, width), lambda i: (i, 0))
    out_shape = [jax.ShapeDtypeStruct((n, width), BF16)]
    out_specs = [row_spec]
    if emit_vn:
        out_shape.append(jax.ShapeDtypeStruct((n, width), F32))
        out_specs.append(row_spec)
    res = pl.pallas_call(
        functools.partial(_sgu_kernel, r=r, gd=width // groups),
        out_shape=out_shape,
        grid=(n // rb,),
        in_specs=[
            row_spec, row_spec,
            pl.BlockSpec((1, width), lambda i: (0, 0)),
            pl.BlockSpec((groups, r, r), lambda i: (0, 0, 0)),
            pl.BlockSpec((r, width), lambda i: (0, 0)),
        ],
        out_specs=out_specs,
        compiler_params=_params(1),
        name="spatial_gating",
    )(u, gv, g_sgu.reshape(1, width), w_mix, bias)
    return res if emit_vn else (res[0], None)


def _gla_kernel(*refs, lc, has_state, q_scale):
    if has_state:
        (q_ref, k_ref, v_ref, r_ref, a_ref, wgh_ref, wgl_ref, bg_ref, gg_ref, s0_ref,
         o_ref, s_ref, st_ref, lg_ref) = refs
    else:
        (q_ref, k_ref, v_ref, r_ref, a_ref, wgh_ref, wgl_ref, bg_ref, gg_ref,
         o_ref, s_ref, st_ref, lg_ref) = refs
    tb = pl.program_id(2)
    n_rows, dk = q_ref.shape

    @pl.when(tb == 0)
    def _():
        if has_state:
            st_ref[...] = s0_ref[...].T
        else:
            st_ref[...] = jnp.zeros_like(st_ref)

    a_hi, a_lo = _split_bf16(a_ref[...])
    pre = (_dot(a_hi, wgh_ref[...]) + _dot(a_lo, wgh_ref[...]) + _dot(a_hi, wgl_ref[...])
           + bg_ref[...])
    lg_ref[...] = -_softplus(-pre) * (1.0 / GLA_TAU)

    row = lax.broadcasted_iota(jnp.int32, (lc, lc), 0)
    col = lax.broadcasted_iota(jnp.int32, (lc, lc), 1)
    lower = jnp.where(col <= row, 1.0, 0.0).astype(BF16)

    def chunk(c, _):
        r0 = pl.multiple_of(c * lc, lc)
        lg_hi, lg_lo = _split_bf16(lg_ref[pl.ds(r0, lc), :])
        b = _dot(lower, lg_hi) + _dot(lower, lg_lo)
        q = q_ref[pl.ds(r0, lc), :].astype(F32) * q_scale
        k = k_ref[pl.ds(r0, lc), :].astype(F32)
        v = v_ref[pl.ds(r0, lc), :]

        scores = jnp.zeros((lc, lc), F32)
        h = lc // 2
        while h >= GLA_DIRECT:
            b3 = b.reshape(lc // (2 * h), 2 * h, dk)
            f = jnp.exp(-jnp.abs(b3 - b3[:, h - 1:h, :])).reshape(lc, dk)
            p = _dot_nt((q * f).astype(BF16), (k * f).astype(BF16))
            mask = ((row // h) == (col // h) + 1) & ((col // h) % 2 == 0)
            scores = scores + jnp.where(mask, p, 0.0)
            h //= 2
        for d in range(min(GLA_DIRECT, lc)):
            if d == 0:
                pd = jnp.sum(q * k, axis=-1, keepdims=True)
            else:
                kd = pltpu.roll(k, d, 0)
                bd = pltpu.roll(b, d, 0)
                pd = jnp.sum(q * kd * jnp.exp(jnp.minimum(b - bd, 0.0)), axis=-1, keepdims=True)
            mask = (col == row - d) & ((row % GLA_DIRECT) >= d)
            scores = scores + jnp.where(mask, pd, 0.0)

        st = st_ref[...]
        o = _dot_nt((q * jnp.exp(b)).astype(BF16), st.astype(BF16))
        o = o + _dot(scores.astype(BF16), v)
        b_last = b[lc - 1:lc, :]
        k_dec = (k * jnp.exp(b_last - b)).astype(BF16)
        st_ref[...] = st * jnp.exp(b_last) + _dot_tn(v, k_dec)

        on = _rms(o, gg_ref[...])
        o_ref[pl.ds(r0, lc), :] = (on * _silu(r_ref[pl.ds(r0, lc), :].astype(F32))).astype(o_ref.dtype)
        return 0

    lax.fori_loop(0, n_rows // lc, chunk, 0)

    @pl.when(tb == pl.num_programs(2) - 1)
    def _():
        s_ref[...] = st_ref[...].T


def _gla(q, k, v, r, a, wg_hi, wg_lo, b_gate, g_gla, state, layer_i, n_heads, tb_pref=512):
    b, l, wq = q.shape
    dk = wq // n_heads
    dv = v.shape[2] // n_heads
    ra = a.shape[2]
    lc = min(GLA_CHUNK, l)
    tb = min(tb_pref, l)
    assert l % tb == 0 and tb % lc == 0 and lc % GLA_DIRECT == 0
    qk_spec = pl.BlockSpec((None, tb, dk), lambda bi, h, t: (bi, t, h))
    vr_spec = pl.BlockSpec((None, tb, dv), lambda bi, h, t: (bi, t, h))
    in_specs = [
        qk_spec, qk_spec, vr_spec, vr_spec,
        pl.BlockSpec((None, tb, ra), lambda bi, h, t: (bi, t, 0)),
        pl.BlockSpec((ra, dk), lambda bi, h, t: (0, h)),
        pl.BlockSpec((ra, dk), lambda bi, h, t: (0, h)),
        pl.BlockSpec((1, dk), lambda bi, h, t: (0, h)),
        pl.BlockSpec((1, dv), lambda bi, h, t: (0, 0)),
    ]
    args = [q, k, v, r, a, wg_hi, wg_lo, b_gate.reshape(1, wq), g_gla.reshape(1, dv)]
    has_state = state is not None
    if has_state:
        in_specs.append(
            pl.BlockSpec((None, None, None, dk, dv), lambda bi, h, t: (layer_i, bi, h, 0, 0)))
        args.append(state)
    return pl.pallas_call(
        functools.partial(_gla_kernel, lc=lc, has_state=has_state, q_scale=dk ** -0.5),
        out_shape=[jax.ShapeDtypeStruct((b, l, n_heads * dv), BF16),
                   jax.ShapeDtypeStruct((b, n_heads, dk, dv), F32)],
        grid=(b, n_heads, l // tb),
        in_specs=in_specs,
        out_specs=[vr_spec, pl.BlockSpec((None, None, dk, dv), lambda bi, h, t: (bi, h, 0, 0))],
        scratch_shapes=[pltpu.VMEM((dv, dk), F32), pltpu.VMEM((tb, dk), F32)],
        compiler_params=_params(3),
        name="gated_linear_attention",
    )(*args)


def _trunk(x3, grp, mod, weights, dims, cache_k, cache_v, state_gla):
    (g_norm, w_in_even, w_out_even, g_sgu, sgu_mix, sgu_bias, w_in_odd, w_a, wg, b_gate, g_gla,
     w_out_odd, w_ff_up, w_ff_down) = weights
    sb_heads, sb_width, sgu_width, gla_heads, gla_qk, gla_vw = dims
    n_seq, seq_len, d = x3.shape
    sample = cache_k is not None
    x = x3.reshape(n_seq * seq_len, d)
    depth = g_norm.shape[0]
    ks, vs, gvs, states = [], [], [], []
    for layer in range(depth):
        i = layer // 2
        if layer % 2 == 0:
            q, k, v, u, gv = _inproj(
                x, mod, layer, g_norm[layer, 0], w_in_even[i],
                [(sb_width, BF16), (sb_width, F32), (sb_width, F32), (sgu_width, BF16), (sgu_width, F32)],
                grp)
            seq = lambda t: t.reshape(n_seq, seq_len, t.shape[-1])
            if sample:
                o_a = _sb_sample(seq(q), seq(k), seq(v), cache_k, cache_v, i, sb_heads)
            else:
                o_a = _sb_prompt(seq(q), seq(k), seq(v), sb_heads)
            o_b, vn = _sgu(u, gv, g_sgu[i], sgu_mix[i], sgu_bias[i], emit_vn=sample)
            acts = [o_a.reshape(x.shape[0], sb_width), o_b]
            ws = [w_out_even[i, :sb_width], w_out_even[i, sb_width:]]
            ks.append(k)
            vs.append(v)
            gvs.append(vn)
        else:
            q, k, v, r, a = _inproj(
                x, mod, layer, g_norm[layer, 0], w_in_odd[i],
                [(gla_qk, BF16), (gla_qk, BF16), (gla_vw, BF16), (gla_vw, BF16)],
                grp, aux_w=w_a[i])
            seq = lambda t: t.reshape(n_seq, seq_len, t.shape[-1])
            o, s_new = _gla(seq(q), seq(k), seq(v), seq(r), seq(a), wg[0][i], wg[1][i], b_gate[i],
                            g_gla[i], state_gla if sample else None, i, gla_heads)
            acts = [o.reshape(x.shape[0], gla_vw)]
            ws = [w_out_odd[i]]
            states.append(s_new)
        x = _outproj(acts, ws, x, mod, layer, g_norm[layer, 1], grp)
        x = _ff(x, mod, layer, g_norm[layer, 2], g_norm[layer, 3], w_ff_up[layer], w_ff_down[layer], grp)
    return x.reshape(n_seq, seq_len, d), ks, vs, states, gvs


def kernel(x_prompt, x_sample, cache_sb_k, cache_sb_v, state_gla, c_prompt, c_sample, w_mod, b_mod, g_norm, w_in_even, w_out_even, g_sgu, w_sgu, b_sgu, w_in_odd, w_gate_up, b_gate, g_gla, w_out_odd, w_ff_up, w_ff_down):
    batch, seq, d = x_prompt.shape
    dec_batch, dec_seq, _ = x_sample.shape
    n_even, _, past, sb_heads, sb_dh = cache_sb_k.shape
    sb_width = sb_heads * sb_dh
    _, _, gla_heads, gla_dk, gla_dv = state_gla.shape
    gla_qk, gla_vw = gla_heads * gla_dk, gla_heads * gla_dv
    _, groups, sgu_chunk, _ = w_sgu.shape
    sgu_width = g_sgu.shape[1]
    rank = w_gate_up.shape[1]
    dims = (sb_heads, sb_width, sgu_width, gla_heads, gla_qk, gla_vw)

    mod = _modulation(jnp.concatenate([c_sample, c_prompt], axis=0), w_mod, b_mod)
    mod = mod.reshape(mod.shape[0], mod.shape[1], 1, mod.shape[2])

    cast = lambda t: t.astype(BF16)
    n_main = 2 * gla_qk + 2 * gla_vw
    w_a = jnp.pad(w_in_odd[:, :, n_main:], ((0, 0), (0, 0), (0, LANES - rank)))
    w_a_hi = cast(w_a)
    w_a = [(w_a_hi[i], cast(w_a[i] - w_a_hi[i].astype(F32))) for i in range(w_a.shape[0])]
    wg = jnp.pad(w_gate_up, ((0, 0), (0, LANES - rank), (0, 0)))
    wg_hi = cast(wg)
    wg = (wg_hi, cast(wg - wg_hi.astype(F32)))
    tril = jnp.tril(jnp.ones((sgu_chunk, sgu_chunk), F32))
    w_tri = w_sgu * tril

    def sgu_tables(chunk_len, n_chunks):
        wt = w_tri[:, :, :chunk_len, :chunk_len]
        eye = jnp.eye(n_chunks, dtype=F32)
        mix = jnp.einsum('ab,ngts->ngatbs', eye, wt).reshape(
            n_even, groups, n_chunks * chunk_len, n_chunks * chunk_len)
        bias = jnp.repeat(jnp.swapaxes(b_sgu[:, :, :chunk_len], 1, 2), sgu_width // groups, axis=2)
        return cast(mix), jnp.tile(bias, (1, n_chunks, 1))

    common = (g_norm, cast(w_in_even), cast(w_out_even), g_sgu)
    tail = (cast(w_in_odd[:, :, :n_main]), w_a, wg, b_gate, g_gla, cast(w_out_odd),
            cast(w_ff_up), cast(w_ff_down))

    grp_p = _Group(batch, seq, dec_batch, 512)
    grp_s = _Group(dec_batch, dec_seq, 0, 512)

    y_p, ks_p, vs_p, st_p, _ = _trunk(
        x_prompt, grp_p, mod, common + sgu_tables(sgu_chunk, 1) + tail, dims, None, None, None)
    y_s, ks_s, vs_s, st_s, gv_s = _trunk(
        x_sample, grp_s, mod, common + sgu_tables(dec_seq, grp_s.tm // dec_seq) + tail, dims,
        cache_sb_k.reshape(n_even, dec_batch, past, sb_width),
        cache_sb_v.reshape(n_even, dec_batch, past, sb_width), state_gla)

    heads = lambda ts, b, l: jnp.stack(ts).reshape(len(ts), b, l, sb_heads, sb_dh)
    return (y_p, y_s,
            heads(ks_p, batch, seq), heads(vs_p, batch, seq), jnp.stack(st_p),
            heads(ks_s, dec_batch, dec_seq), heads(vs_s, dec_batch, dec_seq), jnp.stack(st_s),
            jnp.stack(gv_s).reshape(len(gv_s), dec_batch, dec_seq, sgu_width))
```

```python
import functools

import jax
import jax.numpy as jnp
from jax import lax
from jax.experimental import pallas as pl
from jax.experimental.pallas import tpu as pltpu

F32 = jnp.float32
BF16 = jnp.bfloat16

EPS = 1e-6
GLA_CHUNK = 64
GLA_TAU = 16.0
SUBLANES = 8
LANES = 128
MASKED_LOG = -1e30
VMEM_LIMIT_BYTES = 56 * 1024 * 1024


def _params(n_grid):
    return pltpu.CompilerParams(
        dimension_semantics=("arbitrary",) * n_grid,
        vmem_limit_bytes=VMEM_LIMIT_BYTES)


def _dot(a, b):
    return jnp.dot(a, b, preferred_element_type=F32)


def _dot_nt(a, b):
    return lax.dot_general(a, b, (((1,), (1,)), ((), ())), preferred_element_type=F32)


def _dot_tn(a, b):
    return lax.dot_general(a, b, (((0,), (0,)), ((), ())), preferred_element_type=F32)


def _split_bf16(x):
    hi = x.astype(BF16)
    lo = (x - hi.astype(F32)).astype(BF16)
    return hi, lo


def _rms(x, g):
    return x * lax.rsqrt(jnp.mean(x * x, axis=-1, keepdims=True) + EPS) * g


def _softplus(z):
    return jnp.maximum(z, 0.0) + jnp.log(1.0 + jnp.exp(-jnp.abs(z)))


def _silu(x):
    return x * jax.nn.sigmoid(x)


def _rows(ref, rows):
    v = ref[...]
    nb, _, d = v.shape
    if nb == 1:
        return v[0]
    return jnp.broadcast_to(v, (nb, rows, d)).reshape(nb * rows, d)


class _Group:
    def __init__(self, n_seq, seq_len, mod_row0, tm_pref):
        self.n_seq, self.seq_len, self.mod_row0 = n_seq, seq_len, mod_row0
        self.n = n_seq * seq_len
        if seq_len >= tm_pref:
            assert seq_len % tm_pref == 0
            self.tm, self.nb = tm_pref, 1
        else:
            self.nb = min(n_seq, tm_pref // seq_len)
            assert n_seq % self.nb == 0
            self.tm = self.nb * seq_len
        self.rows = self.tm // self.nb
        self.blocks_per_seq = max(1, seq_len // self.tm)

    def mod_spec(self, layer, col, d):
        nb, bps, row0 = self.nb, self.blocks_per_seq, self.mod_row0

        def index(i, *_):
            if nb == 1:
                return (layer, row0 + i // bps, 0, col)
            return (layer, row0 // nb + i, 0, col)

        return pl.BlockSpec((None, nb, 1, d), index)


def _mod_kernel(c_ref, w_ref, b_ref, o_ref):
    cs = _silu(c_ref[...]).astype(BF16)
    o_ref[...] = _dot(cs, w_ref[...].astype(BF16)) + b_ref[...]


def _modulation(c_all, w_mod, b_mod):
    depth, d, n_out = w_mod.shape
    n_c = c_all.shape[0]
    tn = min(n_out, 1024)
    return pl.pallas_call(
        _mod_kernel,
        out_shape=jax.ShapeDtypeStruct((depth, n_c, n_out), F32),
        grid=(depth, n_out // tn),
        in_specs=[
            pl.BlockSpec((n_c, d), lambda l, j: (0, 0)),
            pl.BlockSpec((None, d, tn), lambda l, j: (l, 0, j)),
            pl.BlockSpec((None, 1, tn), lambda l, j: (l, 0, j)),
        ],
        out_specs=pl.BlockSpec((None, n_c, tn), lambda l, j: (l, 0, j)),
        compiler_params=_params(2),
        name="modulation",
    )(c_all, w_mod, b_mod.reshape(depth, 1, n_out))


def _inproj_kernel(*refs, seg_blocks, seg_scales, rows, has_aux):
    x_ref, g_ref, sh_ref, sc_ref, w_ref = refs[:5]
    pos = 5
    if has_aux:
        wah_ref, wal_ref = refs[5:7]
        pos = 7
    n_seg = len(seg_blocks)
    out_refs = refs[pos:pos + n_seg]
    pos += n_seg
    if has_aux:
        aux_ref = refs[pos]
        pos += 1
    h_ref = refs[pos]
    j = pl.program_id(1)

    @pl.when(j == 0)
    def _():
        h = _rms(x_ref[...], g_ref[...]) * (1.0 + _rows(sc_ref, rows)) + _rows(sh_ref, rows)
        h_hi, h_lo = _split_bf16(h)
        h_ref[...] = h_hi
        if has_aux:
            aux_ref[...] = (_dot(h_hi, wah_ref[...]) + _dot(h_lo, wah_ref[...])
                            + _dot(h_hi, wal_ref[...]))

    j0 = 0
    for o_ref, nblk, scale in zip(out_refs, seg_blocks, seg_scales):
        @pl.when((j >= j0) & (j < j0 + nblk))
        def _(o_ref=o_ref, scale=scale):
            y = _dot(h_ref[...], w_ref[...])
            if scale != 1.0:
                y = y * scale
            o_ref[...] = y.astype(o_ref.dtype)
        j0 += nblk


def _inproj(x, mod, layer, g, w, layer_i, segs, grp, aux_w=None, tn_pref=1024):
    n, d = x.shape
    tm = grp.tm
    tn = min(tn_pref, min(wd for wd, _, _ in segs))
    assert all(wd % tn == 0 for wd, _, _ in segs)
    seg_blocks = tuple(wd // tn for wd, _, _ in segs)
    n_out = sum(wd for wd, _, _ in segs)
    assert w.shape[1] == d and w.shape[2] >= n_out
    in_specs = [
        pl.BlockSpec((tm, d), lambda i, j: (i, 0)),
        pl.BlockSpec((1, d), lambda i, j: (0, 0)),
        grp.mod_spec(layer, 0, d),
        grp.mod_spec(layer, 1, d),
        pl.BlockSpec((None, d, tn), lambda i, j: (layer_i, 0, j)),
    ]
    args = [x, g.reshape(1, d), mod, mod, w]
    out_shape, out_specs = [], []
    j0 = 0
    for (wd, dt, _), nblk in zip(segs, seg_blocks):
        out_shape.append(jax.ShapeDtypeStruct((n, wd), dt))
        out_specs.append(pl.BlockSpec(
            (tm, tn), lambda i, j, j0=j0, nblk=nblk: (i, jnp.clip(j - j0, 0, nblk - 1))))
        j0 += nblk
    has_aux = aux_w is not None
    if has_aux:
        wah, wal = aux_w
        wa = wah.shape[2]
        in_specs += [pl.BlockSpec((None, d, wa), lambda i, j: (layer_i, 0, 0))] * 2
        args += [wah, wal]
        out_shape.append(jax.ShapeDtypeStruct((n, wa), F32))
        out_specs.append(pl.BlockSpec((tm, wa), lambda i, j: (i, 0)))
    return pl.pallas_call(
        functools.partial(_inproj_kernel, seg_blocks=seg_blocks,
                          seg_scales=tuple(sc for _, _, sc in segs), rows=grp.rows, has_aux=has_aux),
        out_shape=out_shape,
        grid=(n // tm, n_out // tn),
        in_specs=in_specs,
        out_specs=out_specs,
        scratch_shapes=[pltpu.VMEM((tm, d), BF16)],
        compiler_params=_params(2),
        name="in_proj",
    )(*args)


def _outproj_kernel(*refs, n_a, rows):
    a_refs = refs[:n_a]
    w_refs = refs[n_a:2 * n_a]
    x_ref, g_ref, gt_ref, o_ref = refs[2 * n_a:]
    y = _dot(a_refs[0][...], w_refs[0][...])
    for a_ref, w_ref in zip(a_refs[1:], w_refs[1:]):
        y = y + _dot(a_ref[...], w_ref[...])
    o_ref[...] = x_ref[...] + _rows(gt_ref, rows) * _rms(y, g_ref[...])


def _outproj(acts, w, layer_i, x, mod, layer, g, grp):
    n, d = x.shape
    tm = grp.tm
    ka = acts[0].shape[1]
    assert all(a.shape[1] == ka for a in acts) and w.shape[1] == ka * len(acts)
    in_specs = [pl.BlockSpec((tm, ka), lambda i: (i, 0)) for _ in acts]
    in_specs += [pl.BlockSpec((None, ka, d), lambda i, r=r: (layer_i, r, 0)) for r in range(len(acts))]
    in_specs += [
        pl.BlockSpec((tm, d), lambda i: (i, 0)),
        pl.BlockSpec((1, d), lambda i: (0, 0)),
        grp.mod_spec(layer, 2, d),
    ]
    return pl.pallas_call(
        functools.partial(_outproj_kernel, n_a=len(acts), rows=grp.rows),
        out_shape=jax.ShapeDtypeStruct((n, d), F32),
        grid=(n // tm,),
        in_specs=in_specs,
        out_specs=pl.BlockSpec((tm, d), lambda i: (i, 0)),
        compiler_params=_params(1),
        name="out_proj",
    )(*acts, *([w] * len(acts)), x, g.reshape(1, d), mod)


def _ff_kernel(x_ref, g1_ref, sh_ref, sc_ref, wu_ref, wd_ref, g2_ref, gt_ref, o_ref,
               h_ref, t_ref, y_ref, *, rows, n_a, n_b):
    j = pl.program_id(1)
    tf = t_ref.shape[2]
    tn = y_ref.shape[2]

    @pl.when(j == 0)
    def _():
        h = _rms(x_ref[...], g1_ref[...]) * (1.0 + _rows(sc_ref, rows)) + _rows(sh_ref, rows)
        h_ref[...] = h.astype(BF16)

    @pl.when(j < n_a)
    def _():
        t = jnp.maximum(_dot(h_ref[...], wu_ref[...]), 0.0)
        t_ref[j] = (t * t).astype(BF16)

    @pl.when(j >= n_a)
    def _():
        acc = _dot(t_ref[0], wd_ref[0:tf, :])
        for a in range(1, n_a):
            acc = acc + _dot(t_ref[a], wd_ref[a * tf:(a + 1) * tf, :])
        y_ref[j - n_a] = acc

    @pl.when(j == n_a + n_b - 1)
    def _():
        ss = jnp.sum(jnp.square(y_ref[0]), axis=-1, keepdims=True)
        for b in range(1, n_b):
            ss = ss + jnp.sum(jnp.square(y_ref[b]), axis=-1, keepdims=True)
        rs = lax.rsqrt(ss * (1.0 / (n_b * tn)) + EPS)
        gate = _rows(gt_ref, rows)
        g2 = g2_ref[...]
        for b in range(n_b):
            cs = slice(b * tn, (b + 1) * tn)
            o_ref[:, cs] = x_ref[:, cs] + gate[:, cs] * (y_ref[b] * rs * g2[:, cs])


def _ff(x, mod, layer, g1, g2, w_up, w_down_blocks, grp):
    n, d = x.shape
    f = w_up.shape[2]
    n_b, tn = w_down_blocks.shape[1], w_down_blocks.shape[3]
    tm = grp.tm
    tf = min(1024, f)
    assert f % tf == 0 and n_b * tn == d
    n_a = f // tf
    return pl.pallas_call(
        functools.partial(_ff_kernel, rows=grp.rows, n_a=n_a, n_b=n_b),
        out_shape=jax.ShapeDtypeStruct((n, d), F32),
        grid=(n // tm, n_a + n_b),
        in_specs=[
            pl.BlockSpec((tm, d), lambda i, j: (i, 0)),
            pl.BlockSpec((1, d), lambda i, j: (0, 0)),
            grp.mod_spec(layer, 3, d),
            grp.mod_spec(layer, 4, d),
            pl.BlockSpec((None, d, tf), lambda i, j: (layer, 0, jnp.minimum(j, n_a - 1))),
            pl.BlockSpec((None, None, f, tn),
                         lambda i, j: (layer, jnp.clip(j - n_a, 0, n_b - 1), 0, 0)),
            pl.BlockSpec((1, d), lambda i, j: (0, 0)),
            grp.mod_spec(layer, 5, d),
        ],
        out_specs=pl.BlockSpec((tm, d), lambda i, j: (i, 0)),
        scratch_shapes=[pltpu.VMEM((tm, d), BF16), pltpu.VMEM((n_a, tm, tf), BF16),
                        pltpu.VMEM((n_b, tm, tn), F32)],
        compiler_params=_params(2),
        name="channel_mlp",
    )(x, g1.reshape(1, d), mod, mod, w_up, w_down_blocks, g2.reshape(1, d), mod)


def _strict_upper(n):
    r = lax.broadcasted_iota(jnp.int32, (n, n), 0)
    c = lax.broadcasted_iota(jnp.int32, (n, n), 1)
    return jnp.where(r > c, 1.0, 0.0).astype(BF16)


def _sb_prompt_kernel(q_ref, k_ref, v_ref, o_ref, kb_ref, vb_ref, lb_ref, hi_ref, lo_ref, *, t, dh):
    qi = pl.program_id(2)
    n_h = q_ref.shape[1] // dh

    @pl.when(qi == 0)
    def _():
        kb_ref[...] = k_ref[...].astype(BF16)
        vb_ref[...] = v_ref[...].astype(BF16)

    u = _strict_upper(t)
    qs = [q_ref[:, h * dh:(h + 1) * dh] for h in range(n_h)]

    def scores(start, slot, diagonal):
        sums = []
        for h in range(n_h):
            z = _dot_nt(qs[h], kb_ref[pl.ds(start, t), h * dh:(h + 1) * dh])
            sp = _softplus(z)
            if diagonal:
                r = lax.broadcasted_iota(jnp.int32, (t, t), 0)
                c = lax.broadcasted_iota(jnp.int32, (t, t), 1)
                neg = jnp.where(c < r, -sp, 0.0)
                log_beta = jnp.where(c < r, z - sp, MASKED_LOG)
            else:
                neg = -sp
                log_beta = z - sp
            lb_ref[slot, h] = log_beta
            hi_ref[slot, h], lo_ref[slot, h] = _split_bf16(neg)
            sums.append(jnp.sum(neg, axis=-1, keepdims=True))
        return tuple(sums)

    def weights(start, slot, carries, accs):
        new_accs = []
        for h in range(n_h):
            suffix = _dot(hi_ref[slot, h], u) + _dot(lo_ref[slot, h], u)
            w = jnp.exp(lb_ref[slot, h] + suffix + carries[h])
            new_accs.append(
                accs[h] + _dot(w.astype(BF16), vb_ref[pl.ds(start, t), h * dh:(h + 1) * dh]))
        return tuple(new_accs)

    sums = scores(pl.multiple_of(qi * t, t), 0, True)
    carries = (jnp.zeros((t, 1), F32),) * n_h
    accs = (jnp.zeros((t, dh), F32),) * n_h

    def body(it, state):
        carries, accs, sums = state
        slot = it % 2
        accs = weights(pl.multiple_of((qi - it) * t, t), slot, carries, accs)
        new_sums = scores(pl.multiple_of((qi - 1 - it) * t, t), 1 - slot, False)
        carries = tuple(c + s for c, s in zip(carries, sums))
        return carries, accs, new_sums

    carries, accs, _ = lax.fori_loop(0, qi, body, (carries, accs, sums))
    accs = weights(0, qi % 2, carries, accs)
    for h in range(n_h):
        o_ref[:, h * dh:(h + 1) * dh] = accs[h].astype(o_ref.dtype)


def _sb_prompt(q, k, v, n_heads, t_pref=256, heads_per_step=4):
    b, l, w = q.shape
    dh = w // n_heads
    t = min(t_pref, l)
    hs = min(heads_per_step, n_heads)
    assert l % t == 0 and n_heads % hs == 0
    q_spec = pl.BlockSpec((None, t, hs * dh), lambda bi, h, qi: (bi, qi, h))
    kv_spec = pl.BlockSpec((None, l, hs * dh), lambda bi, h, qi: (bi, 0, h))
    return pl.pallas_call(
        functools.partial(_sb_prompt_kernel, t=t, dh=dh),
        out_shape=jax.ShapeDtypeStruct((b, l, w), BF16),
        grid=(b, n_heads // hs, l // t),
        in_specs=[q_spec, kv_spec, kv_spec],
        out_specs=q_spec,
        scratch_shapes=[pltpu.VMEM((l, hs * dh), BF16), pltpu.VMEM((l, hs * dh), BF16),
                        pltpu.VMEM((2, hs, t, t), F32), pltpu.VMEM((2, hs, t, t), BF16),
                        pltpu.VMEM((2, hs, t, t), BF16)],
        compiler_params=_params(3),
        name="stick_breaking_prompt",
    )(q, k, v)


def _sb_sample_kernel(q_ref, kn_ref, vn_ref, kc_ref, vc_ref, o_ref, *, tk, n_heads):
    lq, w = q_ref.shape
    dh = kc_ref.shape[1]
    past = kc_ref.shape[0] // n_heads
    nl = n_heads * lq

    qf = q_ref[...].astype(F32)
    row_h = lax.broadcasted_iota(jnp.int32, (nl, w), 0) // lq
    col_h = lax.broadcasted_iota(jnp.int32, (nl, w), 1) // dh
    q_exp = jnp.where(row_h == col_h, jnp.concatenate([qf] * n_heads, axis=0), 0.0).astype(BF16)

    def strict_lower_t(n):
        r = lax.broadcasted_iota(jnp.int32, (n, n), 0)
        c = lax.broadcasted_iota(jnp.int32, (n, n), 1)
        return jnp.where(c > r, 1.0, 0.0).astype(BF16)

    def block(k_heads, v_heads, carry, accs, new):
        n = k_heads[0].shape[0]
        z = _dot_nt(k_heads[0], q_exp[:, 0:dh])
        for h in range(1, n_heads):
            z = z + _dot_nt(k_heads[h], q_exp[:, h * dh:(h + 1) * dh])
        sp = _softplus(z)
        if new:
            key = lax.broadcasted_iota(jnp.int32, (n, nl), 0)
            qry = lax.broadcasted_iota(jnp.int32, (n, nl), 1) % lq
            mask = key < qry
            neg = jnp.where(mask, -sp, 0.0)
        else:
            neg = -sp
        hi, lo = _split_bf16(neg)
        ut = strict_lower_t(n)
        suffix = _dot(ut, hi) + _dot(ut, lo)
        wgt = jnp.exp(z - sp + suffix + carry)
        if new:
            wgt = jnp.where(mask, wgt, 0.0)
        wt = wgt.T.astype(BF16)
        accs = [acc + _dot(wt[h * lq:(h + 1) * lq, :], v_heads[h]) for h, acc in enumerate(accs)]
        carry = carry + jnp.sum(neg, axis=0, keepdims=True)
        return carry, accs

    pad = jnp.zeros((LANES - lq, w), F32)
    kn = jnp.concatenate([kn_ref[...], pad], axis=0).astype(BF16)
    vn = jnp.concatenate([vn_ref[...], pad], axis=0).astype(BF16)
    heads = lambda a: [a[:, h * dh:(h + 1) * dh] for h in range(n_heads)]
    carry, accs = block(heads(kn), heads(vn), jnp.zeros((1, nl), F32),
                        [jnp.zeros((lq, dh), F32)] * n_heads, True)
    for i in reversed(range(past // tk)):
        rows = [pl.ds(i * tk * n_heads + h, tk, stride=n_heads) for h in range(n_heads)]
        ks = [kc_ref[r, :].astype(BF16) for r in rows]
        vs = [vc_ref[r, :].astype(BF16) for r in rows]
        carry, accs = block(ks, vs, carry, accs, False)

    o_ref[...] = jnp.concatenate(accs, axis=1).astype(o_ref.dtype)


def _sb_sample(q, k_new, v_new, cache_k, cache_v, layer_i, tk_pref=256):
    b, lq, w = q.shape
    _, _, past, n_heads, dh = cache_k.shape
    tk = min(tk_pref, past)
    assert past % tk == 0 and lq <= LANES
    new_spec = pl.BlockSpec((None, lq, w), lambda bi: (bi, 0, 0))
    cache_spec = pl.BlockSpec((None, None, past * n_heads, dh), lambda bi: (layer_i, bi, 0, 0))
    flat = lambda c: c.reshape(c.shape[0], c.shape[1], past * n_heads, dh)
    return pl.pallas_call(
        functools.partial(_sb_sample_kernel, tk=tk, n_heads=n_heads),
        out_shape=jax.ShapeDtypeStruct((b, lq, w), BF16),
        grid=(b,),
        in_specs=[new_spec, new_spec, new_spec, cache_spec, cache_spec],
        out_specs=new_spec,
        compiler_params=_params(1),
        name="stick_breaking_sample",
    )(q, k_new, v_new, flat(cache_k), flat(cache_v))


def _sgu_kernel(u_ref, gv_ref, g_ref, w_ref, b_ref, o_ref, *rest, r, gd):
    vb_ref = rest[-1]
    rb, width = u_ref.shape
    vn = _rms(jax.nn.gelu(gv_ref[...]), g_ref[...])
    if len(rest) == 2:
        rest[0][...] = vn
    vb_ref[...] = vn.astype(BF16)

    def chunk(s, _):
        rs = pl.ds(pl.multiple_of(s * r, r), r)
        for gi in range(width // gd):
            cs = slice(gi * gd, (gi + 1) * gd)
            mix = _dot(w_ref[gi], vb_ref[rs, cs]) + b_ref[:, cs]
            o_ref[rs, cs] = (jax.nn.gelu(u_ref[rs, cs].astype(F32)) * mix).astype(o_ref.dtype)
        return 0

    lax.fori_loop(0, rb // r, chunk, 0)


def _sgu(u, gv, g_sgu, w_mix, bias, layer_i, emit_vn, rb_pref=512):
    n, width = u.shape
    _, groups, r, _ = w_mix.shape
    rb = max(r, min(rb_pref, n))
    assert n % rb == 0 and rb % r == 0
    row_spec = pl.BlockSpec((rb, width), lambda i: (i, 0))
    out_shape = [jax.ShapeDtypeStruct((n, width), BF16)]
    out_specs = [row_spec]
    if emit_vn:
        out_shape.append(jax.ShapeDtypeStruct((n, width), F32))
        out_specs.append(row_spec)
    res = pl.pallas_call(
        functools.partial(_sgu_kernel, r=r, gd=width // groups),
        out_shape=out_shape,
        grid=(n // rb,),
        in_specs=[
            row_spec, row_spec,
            pl.BlockSpec((1, width), lambda i: (0, 0)),
            pl.BlockSpec((None, groups, r, r), lambda i: (layer_i, 0, 0, 0)),
            pl.BlockSpec((None, r, width), lambda i: (layer_i, 0, 0)),
        ],
        out_specs=out_specs,
        scratch_shapes=[pltpu.VMEM((rb, width), BF16)],
        compiler_params=_params(1),
        name="spatial_gating",
    )(u, gv, g_sgu.reshape(1, width), w_mix, bias)
    return res if emit_vn else (res[0], None)


def _gla_kernel(*refs, lc, has_state, q_scale):
    if has_state:
        (q_ref, k_ref, v_ref, r_ref, a_ref, wgh_ref, wgl_ref, bg_ref, gg_ref, s0_ref,
         o_ref, s_ref, st_ref, lg_ref) = refs
    else:
        (q_ref, k_ref, v_ref, r_ref, a_ref, wgh_ref, wgl_ref, bg_ref, gg_ref,
         o_ref, s_ref, st_ref, lg_ref) = refs
    tb = pl.program_id(2)
    n_rows, dk = q_ref.shape
    n_chunks = n_rows // lc

    @pl.when(tb == 0)
    def _():
        if has_state:
            st_ref[...] = s0_ref[...].T
        else:
            st_ref[...] = jnp.zeros_like(st_ref)

    a_hi, a_lo = _split_bf16(a_ref[...])
    pre = (_dot(a_hi, wgh_ref[...]) + _dot(a_lo, wgh_ref[...]) + _dot(a_hi, wgl_ref[...])
           + bg_ref[...])
    lg_ref[...] = -_softplus(-pre) * (1.0 / GLA_TAU)

    row = lax.broadcasted_iota(jnp.int32, (lc, lc), 0)
    col = lax.broadcasted_iota(jnp.int32, (lc, lc), 1)
    lower = jnp.where(col <= row, 1.0, 0.0).astype(BF16)

    def roll_in_group(x, d):
        x3 = x.reshape(lc // SUBLANES, SUBLANES, x.shape[1])
        return pltpu.roll(x3, d, 1).reshape(x.shape)

    def chunk(c, _):
        r0 = pl.multiple_of(c * lc, lc)
        lg_hi, lg_lo = _split_bf16(lg_ref[pl.ds(r0, lc), :])
        b = _dot(lower, lg_hi) + _dot(lower, lg_lo)
        q = q_ref[pl.ds(r0, lc), :].astype(F32) * q_scale
        k = k_ref[pl.ds(r0, lc), :].astype(F32)
        v = v_ref[pl.ds(r0, lc), :]

        scores = jnp.zeros((lc, lc), F32)
        h = lc // 2
        while h >= SUBLANES:
            b3 = b.reshape(lc // (2 * h), 2 * h, dk)
            f = jnp.exp(-jnp.abs(b3 - b3[:, h - 1:h, :])).reshape(lc, dk)
            p = _dot_nt((q * f).astype(BF16), (k * f).astype(BF16))
            mask = ((row // h) == (col // h) + 1) & ((col // h) % 2 == 0)
            scores = scores + jnp.where(mask, p, 0.0)
            h //= 2
        for d in range(SUBLANES):
            if d == 0:
                pd = jnp.sum(q * k, axis=-1, keepdims=True)
            else:
                kd = roll_in_group(k, d)
                bd = roll_in_group(b, d)
                pd = jnp.sum(q * kd * jnp.exp(jnp.minimum(b - bd, 0.0)), axis=-1, keepdims=True)
            mask = (col == row - d) & ((row % SUBLANES) >= d)
            scores = scores + jnp.where(mask, pd, 0.0)

        st = st_ref[...]
        o = _dot_nt((q * jnp.exp(b)).astype(BF16), st.astype(BF16))
        o = o + _dot(scores.astype(BF16), v)
        b_last = b[lc - 1:lc, :]
        k_dec = (k * jnp.exp(b_last - b)).astype(BF16)
        st_ref[...] = st * jnp.exp(b_last) + _dot_tn(v, k_dec)

        on = _rms(o, gg_ref[...])
        o_ref[pl.ds(r0, lc), :] = (on * _silu(r_ref[pl.ds(r0, lc), :].astype(F32))).astype(o_ref.dtype)
        return 0

    lax.fori_loop(0, n_chunks, chunk, 0, unroll=2 if n_chunks % 2 == 0 else 1)

    @pl.when(tb == pl.num_programs(2) - 1)
    def _():
        s_ref[...] = st_ref[...].T


def _gla(q, k, v, r, a, wg_hi, wg_lo, b_gate, g_gla, state, layer_i, n_heads, tb_pref=512):
    b, l, wq = q.shape
    dk = wq // n_heads
    dv = v.shape[2] // n_heads
    ra = a.shape[2]
    lc = min(GLA_CHUNK, l)
    tb = min(tb_pref, l)
    assert l % tb == 0 and tb % lc == 0 and lc % (2 * SUBLANES) == 0
    qk_spec = pl.BlockSpec((None, tb, dk), lambda bi, h, t: (bi, t, h))
    vr_spec = pl.BlockSpec((None, tb, dv), lambda bi, h, t: (bi, t, h))
    in_specs = [
        qk_spec, qk_spec, vr_spec, vr_spec,
        pl.BlockSpec((None, tb, ra), lambda bi, h, t: (bi, t, 0)),
        pl.BlockSpec((None, ra, dk), lambda bi, h, t: (layer_i, 0, h)),
        pl.BlockSpec((None, ra, dk), lambda bi, h, t: (layer_i, 0, h)),
        pl.BlockSpec((None, 1, dk), lambda bi, h, t: (layer_i, 0, h)),
        pl.BlockSpec((None, 1, dv), lambda bi, h, t: (layer_i, 0, 0)),
    ]
    args = [q, k, v, r, a, wg_hi, wg_lo, b_gate.reshape(-1, 1, wq), g_gla.reshape(-1, 1, dv)]
    has_state = state is not None
    if has_state:
        in_specs.append(
            pl.BlockSpec((None, None, None, dk, dv), lambda bi, h, t: (layer_i, bi, h, 0, 0)))
        args.append(state)
    return pl.pallas_call(
        functools.partial(_gla_kernel, lc=lc, has_state=has_state, q_scale=dk ** -0.5),
        out_shape=[jax.ShapeDtypeStruct((b, l, n_heads * dv), BF16),
                   jax.ShapeDtypeStruct((b, n_heads, dk, dv), F32)],
        grid=(b, n_heads, l // tb),
        in_specs=in_specs,
        out_specs=[vr_spec, pl.BlockSpec((None, None, dk, dv), lambda bi, h, t: (bi, h, 0, 0))],
        scratch_shapes=[pltpu.VMEM((dv, dk), F32), pltpu.VMEM((tb, dk), F32)],
        compiler_params=_params(3),
        name="gated_linear_attention",
    )(*args)


def _trunk(x3, grp, mod, weights, dims, cache_k, cache_v, state_gla):
    (g_norm, w_in_even, w_out_even, g_sgu, sgu_mix, sgu_bias, w_in_odd, w_a, wg, b_gate, g_gla,
     w_out_odd, w_ff_up, w_ff_down) = weights
    sb_heads, sb_width, sgu_width, gla_heads, gla_qk, gla_vw = dims
    n_seq, seq_len, d = x3.shape
    sample = cache_k is not None
    x = x3.reshape(n_seq * seq_len, d)
    seq = lambda t: t.reshape(n_seq, seq_len, t.shape[-1])
    depth = g_norm.shape[0]
    ks, vs, gvs, states = [], [], [], []
    for layer in range(depth):
        i = layer // 2
        if layer % 2 == 0:
            sb_scale = (sb_width // sb_heads) ** -0.5
            q, k, v, u, gv = _inproj(
                x, mod, layer, g_norm[layer, 0], w_in_even, i,
                [(sb_width, BF16, sb_scale), (sb_width, F32, 1.0), (sb_width, F32, 1.0),
                 (sgu_width, BF16, 1.0), (sgu_width, F32, 1.0)], grp)
            if sample:
                o_a = _sb_sample(seq(q), seq(k), seq(v), cache_k, cache_v, i)
            else:
                o_a = _sb_prompt(seq(q), seq(k), seq(v), sb_heads)
            o_b, vn = _sgu(u, gv, g_sgu[i], sgu_mix, sgu_bias, i, emit_vn=sample)
            x = _outproj([o_a.reshape(x.shape[0], sb_width), o_b], w_out_even, i, x, mod, layer,
                         g_norm[layer, 1], grp)
            ks.append(k)
            vs.append(v)
            gvs.append(vn)
        else:
            q, k, v, r, a = _inproj(
                x, mod, layer, g_norm[layer, 0], w_in_odd, i,
                [(gla_qk, BF16, 1.0), (gla_qk, BF16, 1.0), (gla_vw, BF16, 1.0), (gla_vw, BF16, 1.0)],
                grp, aux_w=w_a)
            o, s_new = _gla(seq(q), seq(k), seq(v), seq(r), seq(a), wg[0], wg[1], b_gate, g_gla,
                            state_gla if sample else None, i, gla_heads)
            x = _outproj([o.reshape(x.shape[0], gla_vw)], w_out_odd, i, x, mod, layer,
                         g_norm[layer, 1], grp)
            states.append(s_new)
        x = _ff(x, mod, layer, g_norm[layer, 2], g_norm[layer, 3], w_ff_up, w_ff_down, grp)
    return x.reshape(n_seq, seq_len, d), ks, vs, states, gvs


def kernel(x_prompt, x_sample, cache_sb_k, cache_sb_v, state_gla, c_prompt, c_sample, w_mod, b_mod, g_norm, w_in_even, w_out_even, g_sgu, w_sgu, b_sgu, w_in_odd, w_gate_up, b_gate, g_gla, w_out_odd, w_ff_up, w_ff_down):
    batch, seq, d = x_prompt.shape
    dec_batch, dec_seq, _ = x_sample.shape
    n_even, _, past, sb_heads, sb_dh = cache_sb_k.shape
    sb_width = sb_heads * sb_dh
    _, _, gla_heads, gla_dk, gla_dv = state_gla.shape
    gla_qk, gla_vw = gla_heads * gla_dk, gla_heads * gla_dv
    _, groups, sgu_chunk, _ = w_sgu.shape
    sgu_width = g_sgu.shape[1]
    rank = w_gate_up.shape[1]
    depth, d_ff = w_ff_up.shape[0], w_ff_up.shape[2]
    dims = (sb_heads, sb_width, sgu_width, gla_heads, gla_qk, gla_vw)

    mod = _modulation(jnp.concatenate([c_sample, c_prompt], axis=0), w_mod, b_mod)
    mod = mod.reshape(mod.shape[0], mod.shape[1], 1, mod.shape[2])

    cast = lambda t: t.astype(BF16)
    n_main = 2 * gla_qk + 2 * gla_vw
    w_a = jnp.pad(w_in_odd[:, :, n_main:], ((0, 0), (0, 0), (0, LANES - rank)))
    w_a_hi = cast(w_a)
    w_a = (w_a_hi, cast(w_a - w_a_hi.astype(F32)))
    wg = jnp.pad(w_gate_up, ((0, 0), (0, LANES - rank), (0, 0)))
    wg_hi = cast(wg)
    wg = (wg_hi, cast(wg - wg_hi.astype(F32)))
    w_tri = w_sgu * jnp.tril(jnp.ones((sgu_chunk, sgu_chunk), F32))

    def sgu_tables(chunk_len):
        mix = cast(w_tri[:, :, :chunk_len, :chunk_len])
        bias = jnp.repeat(jnp.swapaxes(b_sgu[:, :, :chunk_len], 1, 2), sgu_width // groups, axis=2)
        return mix, bias

    n_b = max(1, d // 256)
    w_down_blocks = cast(w_ff_down).reshape(depth, d_ff, n_b, d // n_b).transpose(0, 2, 1, 3)

    common = (g_norm, cast(w_in_even), cast(w_out_even), g_sgu)
    tail = (cast(w_in_odd), w_a, wg, b_gate, g_gla, cast(w_out_odd), cast(w_ff_up), w_down_blocks)

    grp_p = _Group(batch, seq, dec_batch, 512)
    grp_s = _Group(dec_batch, dec_seq, 0, 512)

    y_p, ks_p, vs_p, st_p, _ = _trunk(
        x_prompt, grp_p, mod, common + sgu_tables(sgu_chunk) + tail, dims, None, None, None)
    y_s, ks_s, vs_s, st_s, gv_s = _trunk(
        x_sample, grp_s, mod, common + sgu_tables(dec_seq) + tail, dims,
        cache_sb_k, cache_sb_v, state_gla)

    heads = lambda ts, b, l: jnp.stack(ts).reshape(len(ts), b, l, sb_heads, sb_dh)
    return (y_p, y_s,
            heads(ks_p, batch, seq), heads(vs_p, batch, seq), jnp.stack(st_p),
            heads(ks_s, dec_batch, dec_seq), heads(vs_s, dec_batch, dec_seq), jnp.stack(st_s),
            jnp.stack(gv_s).reshape(len(gv_s), dec_batch, dec_seq, sgu_width))
```

```python
import functools

import jax
import jax.numpy as jnp
from jax import lax
from jax.experimental import pallas as pl
from jax.experimental.pallas import tpu as pltpu

F32 = jnp.float32
BF16 = jnp.bfloat16

EPS = 1e-6
GLA_CHUNK = 64
GLA_TAU = 16.0
SUBLANES = 8
LANES = 128
MASKED_LOG = -1e30
VMEM_LIMIT_BYTES = 56 * 1024 * 1024


def _params(n_grid):
    return pltpu.CompilerParams(
        dimension_semantics=("arbitrary",) * n_grid,
        vmem_limit_bytes=VMEM_LIMIT_BYTES)


def _dot(a, b):
    return jnp.dot(a, b, preferred_element_type=F32)


def _dot_nt(a, b):
    return lax.dot_general(a, b, (((1,), (1,)), ((), ())), preferred_element_type=F32)


def _dot_tn(a, b):
    return lax.dot_general(a, b, (((0,), (0,)), ((), ())), preferred_element_type=F32)


def _split_bf16(x):
    hi = x.astype(BF16)
    lo = (x - hi.astype(F32)).astype(BF16)
    return hi, lo


def _rms(x, g):
    return x * lax.rsqrt(jnp.mean(x * x, axis=-1, keepdims=True) + EPS) * g


def _softplus(z):
    return jnp.maximum(z, 0.0) + jnp.log(1.0 + jnp.exp(-jnp.abs(z)))


def _silu(x):
    return x * jax.nn.sigmoid(x)


def _rows(ref, rows):
    v = ref[...]
    nb, _, d = v.shape
    if nb == 1:
        return v[0]
    return jnp.broadcast_to(v, (nb, rows, d)).reshape(nb * rows, d)


class _Group:
    def __init__(self, n_seq, seq_len, mod_row0, tm_pref):
        self.n_seq, self.seq_len, self.mod_row0 = n_seq, seq_len, mod_row0
        self.n = n_seq * seq_len
        if seq_len >= tm_pref:
            assert seq_len % tm_pref == 0
            self.tm, self.nb = tm_pref, 1
        else:
            self.nb = min(n_seq, tm_pref // seq_len)
            assert n_seq % self.nb == 0
            self.tm = self.nb * seq_len
        self.rows = self.tm // self.nb
        self.blocks_per_seq = max(1, seq_len // self.tm)

    def mod_spec(self, layer, col, d):
        nb, bps, row0 = self.nb, self.blocks_per_seq, self.mod_row0

        def index(i, *_):
            if nb == 1:
                return (layer, row0 + i // bps, 0, col)
            return (layer, row0 // nb + i, 0, col)

        return pl.BlockSpec((None, nb, 1, d), index)


def _mod_kernel(c_ref, w_ref, b_ref, o_ref):
    cs = _silu(c_ref[...]).astype(BF16)
    o_ref[...] = _dot(cs, w_ref[...].astype(BF16)) + b_ref[...]


def _modulation(c_all, w_mod, b_mod):
    depth, d, n_out = w_mod.shape
    n_c = c_all.shape[0]
    tn = min(n_out, 1024)
    return pl.pallas_call(
        _mod_kernel,
        out_shape=jax.ShapeDtypeStruct((depth, n_c, n_out), F32),
        grid=(depth, n_out // tn),
        in_specs=[
            pl.BlockSpec((n_c, d), lambda l, j: (0, 0)),
            pl.BlockSpec((None, d, tn), lambda l, j: (l, 0, j)),
            pl.BlockSpec((None, 1, tn), lambda l, j: (l, 0, j)),
        ],
        out_specs=pl.BlockSpec((None, n_c, tn), lambda l, j: (l, 0, j)),
        compiler_params=_params(2),
        name="modulation",
    )(c_all, w_mod, b_mod.reshape(depth, 1, n_out))


def _inproj_kernel(*refs, seg_blocks, seg_scales, rows, has_aux, n_carried):
    x_ref, g_ref, sh_ref, sc_ref, w_ref = refs[:5]
    pos = 5
    if has_aux:
        wah_ref, wal_ref = refs[5:7]
        pos = 7
    pos += n_carried
    n_seg = len(seg_blocks)
    out_refs = refs[pos:pos + n_seg]
    pos += n_seg
    if has_aux:
        aux_ref = refs[pos]
        pos += 1
    h_ref = refs[pos]
    j = pl.program_id(1)

    @pl.when(j == 0)
    def _():
        h = _rms(x_ref[...], g_ref[...]) * (1.0 + _rows(sc_ref, rows)) + _rows(sh_ref, rows)
        h_hi, h_lo = _split_bf16(h)
        h_ref[...] = h_hi
        if has_aux:
            aux_ref[...] = (_dot(h_hi, wah_ref[...]) + _dot(h_lo, wah_ref[...])
                            + _dot(h_hi, wal_ref[...]))

    j0 = 0
    for o_ref, nblk, scale in zip(out_refs, seg_blocks, seg_scales):
        @pl.when((j >= j0) & (j < j0 + nblk))
        def _(o_ref=o_ref, scale=scale):
            y = _dot(h_ref[...], w_ref[...])
            if scale != 1.0:
                y = y * scale
            o_ref[...] = y.astype(o_ref.dtype)
        j0 += nblk


def _inproj(x, mod, layer, g, w, layer_i, segs, grp, aux_w=None, tn_pref=1024):
    n, d = x.shape
    tm = grp.tm
    tn = min(tn_pref, min(sg[0] for sg in segs))
    assert all(sg[0] % tn == 0 for sg in segs)
    seg_blocks = tuple(sg[0] // tn for sg in segs)
    n_out = sum(sg[0] for sg in segs)
    assert w.shape[1] == d and w.shape[2] >= n_out
    in_specs = [
        pl.BlockSpec((tm, d), lambda i, j: (i, 0)),
        pl.BlockSpec((1, d), lambda i, j: (0, 0)),
        grp.mod_spec(layer, 0, d),
        grp.mod_spec(layer, 1, d),
        pl.BlockSpec((None, d, tn), lambda i, j: (layer_i, 0, j)),
    ]
    args = [x, g.reshape(1, d), mod, mod, w]
    has_aux = aux_w is not None
    if has_aux:
        wah, wal = aux_w
        wa = wah.shape[2]
        in_specs += [pl.BlockSpec((None, d, wa), lambda i, j: (layer_i, 0, 0))] * 2
        args += [wah, wal]
    out_shape, out_specs, aliases = [], [], {}
    j0 = 0
    for k, ((wd, dt, _, stack), nblk) in enumerate(zip(segs, seg_blocks)):
        col = lambda j, j0=j0, nblk=nblk: jnp.clip(j - j0, 0, nblk - 1)
        if stack is None:
            out_shape.append(jax.ShapeDtypeStruct((n, wd), dt))
            out_specs.append(pl.BlockSpec((tm, tn), lambda i, j, col=col: (i, col(j))))
        else:
            n_layers, previous = stack
            out_shape.append(jax.ShapeDtypeStruct((n_layers, n, wd), dt))
            out_specs.append(
                pl.BlockSpec((None, tm, tn), lambda i, j, col=col: (layer_i, i, col(j))))
            if previous is not None:
                aliases[len(args)] = k
                in_specs.append(pl.BlockSpec(memory_space=pl.ANY))
                args.append(previous)
        j0 += nblk
    if has_aux:
        out_shape.append(jax.ShapeDtypeStruct((n, wa), F32))
        out_specs.append(pl.BlockSpec((tm, wa), lambda i, j: (i, 0)))
    return pl.pallas_call(
        functools.partial(_inproj_kernel, seg_blocks=seg_blocks,
                          seg_scales=tuple(sg[2] for sg in segs), rows=grp.rows, has_aux=has_aux,
                          n_carried=len(aliases)),
        out_shape=out_shape,
        grid=(n // tm, n_out // tn),
        in_specs=in_specs,
        out_specs=out_specs,
        scratch_shapes=[pltpu.VMEM((tm, d), BF16)],
        input_output_aliases=aliases,
        compiler_params=_params(2),
        name="in_proj",
    )(*args)


def _outproj_kernel(*refs, n_a, rows):
    a_refs = refs[:n_a]
    w_refs = refs[n_a:2 * n_a]
    x_ref, g_ref, gt_ref, o_ref = refs[2 * n_a:]
    y = _dot(a_refs[0][...], w_refs[0][...])
    for a_ref, w_ref in zip(a_refs[1:], w_refs[1:]):
        y = y + _dot(a_ref[...], w_ref[...])
    o_ref[...] = x_ref[...] + _rows(gt_ref, rows) * _rms(y, g_ref[...])


def _outproj(acts, w, layer_i, x, mod, layer, g, grp):
    n, d = x.shape
    tm = grp.tm
    ka = acts[0].shape[1]
    assert all(a.shape[1] == ka for a in acts) and w.shape[1] == ka * len(acts)
    in_specs = [pl.BlockSpec((tm, ka), lambda i: (i, 0)) for _ in acts]
    in_specs += [pl.BlockSpec((None, ka, d), lambda i, r=r: (layer_i, r, 0)) for r in range(len(acts))]
    in_specs += [
        pl.BlockSpec((tm, d), lambda i: (i, 0)),
        pl.BlockSpec((1, d), lambda i: (0, 0)),
        grp.mod_spec(layer, 2, d),
    ]
    return pl.pallas_call(
        functools.partial(_outproj_kernel, n_a=len(acts), rows=grp.rows),
        out_shape=jax.ShapeDtypeStruct((n, d), F32),
        grid=(n // tm,),
        in_specs=in_specs,
        out_specs=pl.BlockSpec((tm, d), lambda i: (i, 0)),
        compiler_params=_params(1),
        name="out_proj",
    )(*acts, *([w] * len(acts)), x, g.reshape(1, d), mod)


def _ff_kernel(x_ref, g1_ref, sh_ref, sc_ref, wu_ref, wd_ref, g2_ref, gt_ref, o_ref,
               h_ref, t_ref, y_ref, *, rows, n_a, n_b):
    j = pl.program_id(1)
    tf = t_ref.shape[2]
    tn = y_ref.shape[2]

    @pl.when(j == 0)
    def _():
        h = _rms(x_ref[...], g1_ref[...]) * (1.0 + _rows(sc_ref, rows)) + _rows(sh_ref, rows)
        h_ref[...] = h.astype(BF16)

    @pl.when(j < n_a)
    def _():
        t = jnp.maximum(_dot(h_ref[...], wu_ref[...]), 0.0)
        t_ref[j] = (t * t).astype(BF16)

    @pl.when(j >= n_a)
    def _():
        acc = _dot(t_ref[0], wd_ref[0:tf, :])
        for a in range(1, n_a):
            acc = acc + _dot(t_ref[a], wd_ref[a * tf:(a + 1) * tf, :])
        y_ref[j - n_a] = acc

    @pl.when(j == n_a + n_b - 1)
    def _():
        ss = jnp.sum(jnp.square(y_ref[0]), axis=-1, keepdims=True)
        for b in range(1, n_b):
            ss = ss + jnp.sum(jnp.square(y_ref[b]), axis=-1, keepdims=True)
        rs = lax.rsqrt(ss * (1.0 / (n_b * tn)) + EPS)
        gate = _rows(gt_ref, rows)
        g2 = g2_ref[...]
        for b in range(n_b):
            cs = slice(b * tn, (b + 1) * tn)
            o_ref[:, cs] = x_ref[:, cs] + gate[:, cs] * (y_ref[b] * rs * g2[:, cs])


def _ff(x, mod, layer, g1, g2, w_up, w_down_blocks, grp):
    n, d = x.shape
    f = w_up.shape[2]
    n_b, tn = w_down_blocks.shape[1], w_down_blocks.shape[3]
    tm = grp.tm
    tf = min(1024, f)
    assert f % tf == 0 and n_b * tn == d
    n_a = f // tf
    return pl.pallas_call(
        functools.partial(_ff_kernel, rows=grp.rows, n_a=n_a, n_b=n_b),
        out_shape=jax.ShapeDtypeStruct((n, d), F32),
        grid=(n // tm, n_a + n_b),
        in_specs=[
            pl.BlockSpec((tm, d), lambda i, j: (i, 0)),
            pl.BlockSpec((1, d), lambda i, j: (0, 0)),
            grp.mod_spec(layer, 3, d),
            grp.mod_spec(layer, 4, d),
            pl.BlockSpec((None, d, tf), lambda i, j: (layer, 0, jnp.minimum(j, n_a - 1))),
            pl.BlockSpec((None, None, f, tn),
                         lambda i, j: (layer, jnp.clip(j - n_a, 0, n_b - 1), 0, 0)),
            pl.BlockSpec((1, d), lambda i, j: (0, 0)),
            grp.mod_spec(layer, 5, d),
        ],
        out_specs=pl.BlockSpec((tm, d), lambda i, j: (i, 0)),
        scratch_shapes=[pltpu.VMEM((tm, d), BF16), pltpu.VMEM((n_a, tm, tf), BF16),
                        pltpu.VMEM((n_b, tm, tn), F32)],
        compiler_params=_params(2),
        name="channel_mlp",
    )(x, g1.reshape(1, d), mod, mod, w_up, w_down_blocks, g2.reshape(1, d), mod)


def _strict_upper(n):
    r = lax.broadcasted_iota(jnp.int32, (n, n), 0)
    c = lax.broadcasted_iota(jnp.int32, (n, n), 1)
    return jnp.where(r > c, 1.0, 0.0).astype(BF16)


def _sb_prompt_kernel(q_ref, k_ref, v_ref, o_ref, kb_ref, vb_ref, lb_ref, hi_ref, lo_ref, *, t, dh):
    qi = pl.program_id(2)
    heads = range(q_ref.shape[1] // dh)
    cols = [slice(h * dh, (h + 1) * dh) for h in heads]

    @pl.when(qi == 0)
    def _():
        kb_ref[...] = k_ref[...].astype(BF16)
        vb_ref[...] = v_ref[...].astype(BF16)

    u = _strict_upper(t)
    qs = [q_ref[:, cs] for cs in cols]

    def logits(start):
        return [_dot_nt(q, kb_ref[pl.ds(start, t), cs]) for q, cs in zip(qs, cols)]

    def stage1(zs, slot, diagonal):
        sps = [_softplus(z) for z in zs]
        if diagonal:
            r = lax.broadcasted_iota(jnp.int32, (t, t), 0)
            c = lax.broadcasted_iota(jnp.int32, (t, t), 1)
            negs = [jnp.where(c < r, -sp, 0.0) for sp in sps]
            log_betas = [jnp.where(c < r, z - sp, MASKED_LOG) for z, sp in zip(zs, sps)]
        else:
            negs = [-sp for sp in sps]
            log_betas = [z - sp for z, sp in zip(zs, sps)]
        for h in heads:
            lb_ref[slot, h] = log_betas[h]
            hi_ref[slot, h], lo_ref[slot, h] = _split_bf16(negs[h])
        return tuple(jnp.sum(neg, axis=-1, keepdims=True) for neg in negs)

    def suffixes(slot):
        return [_dot(hi_ref[slot, h], u) + _dot(lo_ref[slot, h], u) for h in heads]

    def stage2(sufs, start, slot, carries, accs):
        ws = [jnp.exp(lb_ref[slot, h] + sufs[h] + carries[h]).astype(BF16) for h in heads]
        return tuple(accs[h] + _dot(ws[h], vb_ref[pl.ds(start, t), cols[h]]) for h in heads)

    sums = stage1(logits(pl.multiple_of(qi * t, t)), 0, True)
    carries = (jnp.zeros((t, 1), F32),) * len(heads)
    accs = (jnp.zeros((t, dh), F32),) * len(heads)

    def body(it, state):
        carries, accs, sums = state
        slot = it % 2
        sufs = suffixes(slot)
        zs = logits(pl.multiple_of((qi - 1 - it) * t, t))
        accs = stage2(sufs, pl.multiple_of((qi - it) * t, t), slot, carries, accs)
        new_sums = stage1(zs, 1 - slot, False)
        carries = tuple(c + s for c, s in zip(carries, sums))
        return carries, accs, new_sums

    carries, accs, _ = lax.fori_loop(0, qi, body, (carries, accs, sums))
    accs = stage2(suffixes(qi % 2), 0, qi % 2, carries, accs)
    for h in heads:
        o_ref[:, cols[h]] = accs[h].astype(o_ref.dtype)


def _sb_prompt(q, k, v, layer_i, n_heads, t_pref=256, heads_per_step=4):
    b, l, w = q.shape
    dh = w // n_heads
    t = min(t_pref, l)
    hs = min(heads_per_step, n_heads)
    assert l % t == 0 and n_heads % hs == 0
    q_spec = pl.BlockSpec((None, t, hs * dh), lambda bi, h, qi: (bi, qi, h))
    kv_spec = pl.BlockSpec((None, None, l, hs * dh), lambda bi, h, qi: (layer_i, bi, 0, h))
    return pl.pallas_call(
        functools.partial(_sb_prompt_kernel, t=t, dh=dh),
        out_shape=jax.ShapeDtypeStruct((b, l, w), BF16),
        grid=(b, n_heads // hs, l // t),
        in_specs=[q_spec, kv_spec, kv_spec],
        out_specs=q_spec,
        scratch_shapes=[pltpu.VMEM((l, hs * dh), BF16), pltpu.VMEM((l, hs * dh), BF16),
                        pltpu.VMEM((2, hs, t, t), F32), pltpu.VMEM((2, hs, t, t), BF16),
                        pltpu.VMEM((2, hs, t, t), BF16)],
        compiler_params=_params(3),
        name="stick_breaking_prompt",
    )(q, k, v)


def _sb_sample_kernel(q_ref, kn_ref, vn_ref, kc_ref, vc_ref, o_ref, *, tk, n_heads):
    lq, w = q_ref.shape
    dh = kc_ref.shape[1]
    past = kc_ref.shape[0] // n_heads
    nl = n_heads * lq

    qf = q_ref[...].astype(F32)
    row_h = lax.broadcasted_iota(jnp.int32, (nl, w), 0) // lq
    col_h = lax.broadcasted_iota(jnp.int32, (nl, w), 1) // dh
    q_exp = jnp.where(row_h == col_h, jnp.concatenate([qf] * n_heads, axis=0), 0.0).astype(BF16)

    def strict_lower_t(n):
        r = lax.broadcasted_iota(jnp.int32, (n, n), 0)
        c = lax.broadcasted_iota(jnp.int32, (n, n), 1)
        return jnp.where(c > r, 1.0, 0.0).astype(BF16)

    def block(k_heads, v_heads, carry, accs, new):
        n = k_heads[0].shape[0]
        z = _dot_nt(k_heads[0], q_exp[:, 0:dh])
        for h in range(1, n_heads):
            z = z + _dot_nt(k_heads[h], q_exp[:, h * dh:(h + 1) * dh])
        sp = _softplus(z)
        if new:
            key = lax.broadcasted_iota(jnp.int32, (n, nl), 0)
            qry = lax.broadcasted_iota(jnp.int32, (n, nl), 1) % lq
            mask = key < qry
            neg = jnp.where(mask, -sp, 0.0)
        else:
            neg = -sp
        hi, lo = _split_bf16(neg)
        ut = strict_lower_t(n)
        suffix = _dot(ut, hi) + _dot(ut, lo)
        wgt = jnp.exp(z - sp + suffix + carry)
        if new:
            wgt = jnp.where(mask, wgt, 0.0)
        wt = wgt.T.astype(BF16)
        accs = [acc + _dot(wt[h * lq:(h + 1) * lq, :], v_heads[h]) for h, acc in enumerate(accs)]
        carry = carry + jnp.sum(neg, axis=0, keepdims=True)
        return carry, accs

    pad = jnp.zeros((LANES - lq, w), F32)
    kn = jnp.concatenate([kn_ref[...], pad], axis=0).astype(BF16)
    vn = jnp.concatenate([vn_ref[...], pad], axis=0).astype(BF16)
    heads = lambda a: [a[:, h * dh:(h + 1) * dh] for h in range(n_heads)]
    carry, accs = block(heads(kn), heads(vn), jnp.zeros((1, nl), F32),
                        [jnp.zeros((lq, dh), F32)] * n_heads, True)
    for i in reversed(range(past // tk)):
        rows = [pl.ds(i * tk * n_heads + h, tk, stride=n_heads) for h in range(n_heads)]
        ks = [kc_ref[r, :].astype(BF16) for r in rows]
        vs = [vc_ref[r, :].astype(BF16) for r in rows]
        carry, accs = block(ks, vs, carry, accs, False)

    o_ref[...] = jnp.concatenate(accs, axis=1).astype(o_ref.dtype)


def _sb_sample(q, k_new, v_new, cache_k, cache_v, layer_i, tk_pref=256):
    b, lq, w = q.shape
    _, _, past, n_heads, dh = cache_k.shape
    tk = min(tk_pref, past)
    assert past % tk == 0 and lq <= LANES
    new_spec = pl.BlockSpec((None, lq, w), lambda bi: (bi, 0, 0))
    kv_new_spec = pl.BlockSpec((None, None, lq, w), lambda bi: (layer_i, bi, 0, 0))
    cache_spec = pl.BlockSpec((None, None, past * n_heads, dh), lambda bi: (layer_i, bi, 0, 0))
    flat = lambda c: c.reshape(c.shape[0], c.shape[1], past * n_heads, dh)
    return pl.pallas_call(
        functools.partial(_sb_sample_kernel, tk=tk, n_heads=n_heads),
        out_shape=jax.ShapeDtypeStruct((b, lq, w), BF16),
        grid=(b,),
        in_specs=[new_spec, kv_new_spec, kv_new_spec, cache_spec, cache_spec],
        out_specs=new_spec,
        compiler_params=_params(1),
        name="stick_breaking_sample",
    )(q, k_new, v_new, flat(cache_k), flat(cache_v))


def _sgu_kernel(u_ref, gv_ref, g_ref, w_ref, b_ref, o_ref, *rest, r, gd):
    vb_ref = rest[-1]
    rb, width = u_ref.shape
    vn = _rms(jax.nn.gelu(gv_ref[...]), g_ref[...])
    if len(rest) == 2:
        rest[0][...] = vn
    vb_ref[...] = vn.astype(BF16)

    def chunk(s, _):
        rs = pl.ds(pl.multiple_of(s * r, r), r)
        for gi in range(width // gd):
            cs = slice(gi * gd, (gi + 1) * gd)
            mix = _dot(w_ref[gi], vb_ref[rs, cs]) + b_ref[:, cs]
            o_ref[rs, cs] = (jax.nn.gelu(u_ref[rs, cs].astype(F32)) * mix).astype(o_ref.dtype)
        return 0

    lax.fori_loop(0, rb // r, chunk, 0)


def _sgu(u, gv, g_sgu, w_mix, bias, layer_i, emit_vn, rb_pref=512):
    n, width = u.shape
    _, groups, r, _ = w_mix.shape
    rb = max(r, min(rb_pref, n))
    assert n % rb == 0 and rb % r == 0
    row_spec = pl.BlockSpec((rb, width), lambda i: (i, 0))
    out_shape = [jax.ShapeDtypeStruct((n, width), BF16)]
    out_specs = [row_spec]
    if emit_vn:
        out_shape.append(jax.ShapeDtypeStruct((n, width), F32))
        out_specs.append(row_spec)
    res = pl.pallas_call(
        functools.partial(_sgu_kernel, r=r, gd=width // groups),
        out_shape=out_shape,
        grid=(n // rb,),
        in_specs=[
            row_spec, row_spec,
            pl.BlockSpec((1, width), lambda i: (0, 0)),
            pl.BlockSpec((None, groups, r, r), lambda i: (layer_i, 0, 0, 0)),
            pl.BlockSpec((None, r, width), lambda i: (layer_i, 0, 0)),
        ],
        out_specs=out_specs,
        scratch_shapes=[pltpu.VMEM((rb, width), BF16)],
        compiler_params=_params(1),
        name="spatial_gating",
    )(u, gv, g_sgu.reshape(1, width), w_mix, bias)
    return res if emit_vn else (res[0], None)


def _gla_kernel(*refs, lc, has_state, has_carried, q_scale):
    q_ref, k_ref, v_ref, r_ref, a_ref, wgh_ref, wgl_ref, bg_ref, gg_ref = refs[:9]
    s0_ref = refs[9] if has_state else None
    o_ref, s_ref, st_ref, lg_ref = refs[9 + has_state + has_carried:]
    tb = pl.program_id(2)
    n_rows, dk = q_ref.shape
    n_chunks = n_rows // lc

    @pl.when(tb == 0)
    def _():
        if has_state:
            st_ref[...] = s0_ref[...].T
        else:
            st_ref[...] = jnp.zeros_like(st_ref)

    a_hi, a_lo = _split_bf16(a_ref[...])
    pre = (_dot(a_hi, wgh_ref[...]) + _dot(a_lo, wgh_ref[...]) + _dot(a_hi, wgl_ref[...])
           + bg_ref[...])
    lg_ref[...] = -_softplus(-pre) * (1.0 / GLA_TAU)

    row = lax.broadcasted_iota(jnp.int32, (lc, lc), 0)
    col = lax.broadcasted_iota(jnp.int32, (lc, lc), 1)
    lower = jnp.where(col <= row, 1.0, 0.0).astype(BF16)

    chunks = [slice(c * lc, (c + 1) * lc) for c in range(n_chunks)]
    per_chunk = lambda x: [x[cs] for cs in chunks]

    def roll_in_group(x, d):
        x3 = x.reshape(n_rows // SUBLANES, SUBLANES, x.shape[1])
        return pltpu.roll(x3, d, 1).reshape(x.shape)

    lg_hi, lg_lo = _split_bf16(lg_ref[...])
    b = jnp.concatenate(
        [_dot(lower, hi) + _dot(lower, lo) for hi, lo in zip(per_chunk(lg_hi), per_chunk(lg_lo))],
        axis=0)
    q = q_ref[...].astype(F32) * q_scale
    k = k_ref[...].astype(F32)

    scores = [jnp.zeros((lc, lc), F32)] * n_chunks
    h = lc // 2
    while h >= SUBLANES:
        b3 = b.reshape(n_rows // (2 * h), 2 * h, dk)
        f = jnp.exp(-jnp.abs(b3 - b3[:, h - 1:h, :])).reshape(n_rows, dk)
        qf, kf = (q * f).astype(BF16), (k * f).astype(BF16)
        mask = ((row // h) == (col // h) + 1) & ((col // h) % 2 == 0)
        scores = [sc + jnp.where(mask, _dot_nt(qc, kc), 0.0)
                  for sc, qc, kc in zip(scores, per_chunk(qf), per_chunk(kf))]
        h //= 2
    for d in range(SUBLANES):
        if d == 0:
            pd = jnp.sum(q * k, axis=-1, keepdims=True)
        else:
            kd = roll_in_group(k, d)
            bd = roll_in_group(b, d)
            pd = jnp.sum(q * kd * jnp.exp(jnp.minimum(b - bd, 0.0)), axis=-1, keepdims=True)
        mask = (col == row - d) & ((row % SUBLANES) >= d)
        scores = [sc + jnp.where(mask, pc, 0.0) for sc, pc in zip(scores, per_chunk(pd))]

    b_last = jnp.broadcast_to(b.reshape(n_chunks, lc, dk)[:, lc - 1:lc, :], (n_chunks, lc, dk))
    decay = [jnp.exp(bl[0:1, :]) for bl in per_chunk(b_last.reshape(n_rows, dk))]
    qd = per_chunk((q * jnp.exp(b)).astype(BF16))
    k_dec = per_chunk((k * jnp.exp(b_last.reshape(n_rows, dk) - b)).astype(BF16))
    vs = [v_ref[cs, :] for cs in chunks]
    intra = [_dot(sc.astype(BF16), vc) for sc, vc in zip(scores, vs)]
    kv = [_dot_tn(vc, kc) for vc, kc in zip(vs, k_dec)]

    st = st_ref[...]
    o_inter = _dot_nt(qd[0], st.astype(BF16))
    for c, cs in enumerate(chunks):
        o = o_inter + intra[c]
        st = st * decay[c] + kv[c]
        if c + 1 < n_chunks:
            o_inter = _dot_nt(qd[c + 1], st.astype(BF16))
        on = _rms(o, gg_ref[...])
        o_ref[cs, :] = (on * _silu(r_ref[cs, :].astype(F32))).astype(o_ref.dtype)
    st_ref[...] = st

    @pl.when(tb == pl.num_programs(2) - 1)
    def _():
        s_ref[...] = st_ref[...].T


def _gla(q, k, v, r, a, wg_hi, wg_lo, b_gate, g_gla, state, layer_i, n_heads, states_out,
         tb_pref=512):
    b, l, wq = q.shape
    dk = wq // n_heads
    dv = v.shape[2] // n_heads
    ra = a.shape[2]
    lc = min(GLA_CHUNK, l)
    tb = min(tb_pref, l)
    assert l % tb == 0 and tb % lc == 0 and lc % (2 * SUBLANES) == 0
    qk_spec = pl.BlockSpec((None, tb, dk), lambda bi, h, t: (bi, t, h))
    vr_spec = pl.BlockSpec((None, tb, dv), lambda bi, h, t: (bi, t, h))
    in_specs = [
        qk_spec, qk_spec, vr_spec, vr_spec,
        pl.BlockSpec((None, tb, ra), lambda bi, h, t: (bi, t, 0)),
        pl.BlockSpec((None, ra, dk), lambda bi, h, t: (layer_i, 0, h)),
        pl.BlockSpec((None, ra, dk), lambda bi, h, t: (layer_i, 0, h)),
        pl.BlockSpec((None, 1, dk), lambda bi, h, t: (layer_i, 0, h)),
        pl.BlockSpec((None, 1, dv), lambda bi, h, t: (layer_i, 0, 0)),
    ]
    args = [q, k, v, r, a, wg_hi, wg_lo, b_gate.reshape(-1, 1, wq), g_gla.reshape(-1, 1, dv)]
    has_state = state is not None
    if has_state:
        in_specs.append(
            pl.BlockSpec((None, None, None, dk, dv), lambda bi, h, t: (layer_i, bi, h, 0, 0)))
        args.append(state)
    aliases = {}
    if states_out is not None:
        aliases[len(args)] = 1
        in_specs.append(pl.BlockSpec(memory_space=pl.ANY))
        args.append(states_out)
    n_layers = b_gate.shape[0]
    return pl.pallas_call(
        functools.partial(_gla_kernel, lc=lc, has_state=has_state,
                          has_carried=states_out is not None, q_scale=dk ** -0.5),
        out_shape=[jax.ShapeDtypeStruct((b, l, n_heads * dv), BF16),
                   jax.ShapeDtypeStruct((n_layers, b, n_heads, dk, dv), F32)],
        grid=(b, n_heads, l // tb),
        in_specs=in_specs,
        out_specs=[vr_spec, pl.BlockSpec((None, None, None, dk, dv),
                                         lambda bi, h, t: (layer_i, bi, h, 0, 0))],
        scratch_shapes=[pltpu.VMEM((dv, dk), F32), pltpu.VMEM((tb, dk), F32)],
        input_output_aliases=aliases,
        compiler_params=_params(3),
        name="gated_linear_attention",
    )(*args)


def _trunk(x3, grp, mod, weights, dims, cache_k, cache_v, state_gla):
    (g_norm, w_in_even, w_out_even, g_sgu, sgu_mix, sgu_bias, w_in_odd, w_a, wg, b_gate, g_gla,
     w_out_odd, w_ff_up, w_ff_down) = weights
    sb_heads, sb_width, sgu_width, gla_heads, gla_qk, gla_vw = dims
    n_seq, seq_len, d = x3.shape
    sample = cache_k is not None
    x = x3.reshape(n_seq * seq_len, d)
    seq = lambda t: t.reshape(n_seq, seq_len, t.shape[-1])
    depth = g_norm.shape[0]
    n_even = (depth + 1) // 2
    k_all = v_all = states = None
    gvs = []
    for layer in range(depth):
        i = layer // 2
        if layer % 2 == 0:
            sb_scale = (sb_width // sb_heads) ** -0.5
            q, k_all, v_all, u, gv = _inproj(
                x, mod, layer, g_norm[layer, 0], w_in_even, i,
                [(sb_width, BF16, sb_scale, None), (sb_width, F32, 1.0, (n_even, k_all)),
                 (sb_width, F32, 1.0, (n_even, v_all)), (sgu_width, BF16, 1.0, None),
                 (sgu_width, F32, 1.0, None)], grp)
            kv_seq = lambda t: t.reshape(n_even, n_seq, seq_len, sb_width)
            if sample:
                o_a = _sb_sample(seq(q), kv_seq(k_all), kv_seq(v_all), cache_k, cache_v, i)
            else:
                o_a = _sb_prompt(seq(q), kv_seq(k_all), kv_seq(v_all), i, sb_heads)
            o_b, vn = _sgu(u, gv, g_sgu[i], sgu_mix, sgu_bias, i, emit_vn=sample)
            x = _outproj([o_a.reshape(x.shape[0], sb_width), o_b], w_out_even, i, x, mod, layer,
                         g_norm[layer, 1], grp)
            gvs.append(vn)
        else:
            q, k, v, r, a = _inproj(
                x, mod, layer, g_norm[layer, 0], w_in_odd, i,
                [(gla_qk, BF16, 1.0, None), (gla_qk, BF16, 1.0, None), (gla_vw, BF16, 1.0, None),
                 (gla_vw, BF16, 1.0, None)], grp, aux_w=w_a)
            o, states = _gla(seq(q), seq(k), seq(v), seq(r), seq(a), wg[0], wg[1], b_gate, g_gla,
                             state_gla if sample else None, i, gla_heads, states)
            x = _outproj([o.reshape(x.shape[0], gla_vw)], w_out_odd, i, x, mod, layer,
                         g_norm[layer, 1], grp)
        x = _ff(x, mod, layer, g_norm[layer, 2], g_norm[layer, 3], w_ff_up, w_ff_down, grp)
    return x.reshape(n_seq, seq_len, d), k_all, v_all, states, gvs


def kernel(x_prompt, x_sample, cache_sb_k, cache_sb_v, state_gla, c_prompt, c_sample, w_mod, b_mod, g_norm, w_in_even, w_out_even, g_sgu, w_sgu, b_sgu, w_in_odd, w_gate_up, b_gate, g_gla, w_out_odd, w_ff_up, w_ff_down):
    batch, seq, d = x_prompt.shape
    dec_batch, dec_seq, _ = x_sample.shape
    n_even, _, past, sb_heads, sb_dh = cache_sb_k.shape
    sb_width = sb_heads * sb_dh
    _, _, gla_heads, gla_dk, gla_dv = state_gla.shape
    gla_qk, gla_vw = gla_heads * gla_dk, gla_heads * gla_dv
    _, groups, sgu_chunk, _ = w_sgu.shape
    sgu_width = g_sgu.shape[1]
    rank = w_gate_up.shape[1]
    depth, d_ff = w_ff_up.shape[0], w_ff_up.shape[2]
    dims = (sb_heads, sb_width, sgu_width, gla_heads, gla_qk, gla_vw)

    mod = _modulation(jnp.concatenate([c_sample, c_prompt], axis=0), w_mod, b_mod)
    mod = mod.reshape(mod.shape[0], mod.shape[1], 1, mod.shape[2])

    cast = lambda t: t.astype(BF16)
    n_main = 2 * gla_qk + 2 * gla_vw
    w_a = jnp.pad(w_in_odd[:, :, n_main:], ((0, 0), (0, 0), (0, LANES - rank)))
    w_a_hi = cast(w_a)
    w_a = (w_a_hi, cast(w_a - w_a_hi.astype(F32)))
    wg = jnp.pad(w_gate_up, ((0, 0), (0, LANES - rank), (0, 0)))
    wg_hi = cast(wg)
    wg = (wg_hi, cast(wg - wg_hi.astype(F32)))
    w_tri = w_sgu * jnp.tril(jnp.ones((sgu_chunk, sgu_chunk), F32))

    def sgu_tables(chunk_len):
        mix = cast(w_tri[:, :, :chunk_len, :chunk_len])
        bias = jnp.repeat(jnp.swapaxes(b_sgu[:, :, :chunk_len], 1, 2), sgu_width // groups, axis=2)
        return mix, bias

    n_b = max(1, d // 256)
    w_down_blocks = cast(w_ff_down).reshape(depth, d_ff, n_b, d // n_b).transpose(0, 2, 1, 3)

    common = (g_norm, cast(w_in_even), cast(w_out_even), g_sgu)
    tail = (cast(w_in_odd), w_a, wg, b_gate, g_gla, cast(w_out_odd), cast(w_ff_up), w_down_blocks)

    grp_p = _Group(batch, seq, dec_batch, 512)
    grp_s = _Group(dec_batch, dec_seq, 0, 512)

    y_p, ks_p, vs_p, st_p, _ = _trunk(
        x_prompt, grp_p, mod, common + sgu_tables(sgu_chunk) + tail, dims, None, None, None)
    y_s, ks_s, vs_s, st_s, gv_s = _trunk(
        x_sample, grp_s, mod, common + sgu_tables(dec_seq) + tail, dims,
        cache_sb_k, cache_sb_v, state_gla)

    heads = lambda t, b, l: t.reshape(t.shape[0], b, l, sb_heads, sb_dh)
    return (y_p, y_s,
            heads(ks_p, batch, seq), heads(vs_p, batch, seq), st_p,
            heads(ks_s, dec_batch, dec_seq), heads(vs_s, dec_batch, dec_seq), st_s,
            jnp.stack(gv_s).reshape(len(gv_s), dec_batch, dec_seq, sgu_width))
```

```python
import functools

import jax
import jax.numpy as jnp
from jax import lax
from jax.experimental import pallas as pl
from jax.experimental.pallas import tpu as pltpu

F32 = jnp.float32
BF16 = jnp.bfloat16

EPS = 1e-6
GLA_CHUNK = 64
GLA_TAU = 16.0
SUBLANES = 8
LANES = 128
MASKED_LOG = -1e30
VMEM_LIMIT_BYTES = 56 * 1024 * 1024


def _params(n_grid):
    return pltpu.CompilerParams(
        dimension_semantics=("arbitrary",) * n_grid,
        vmem_limit_bytes=VMEM_LIMIT_BYTES)


def _dot(a, b):
    return jnp.dot(a, b, preferred_element_type=F32)


def _dot_nt(a, b):
    return lax.dot_general(a, b, (((1,), (1,)), ((), ())), preferred_element_type=F32)


def _dot_tn(a, b):
    return lax.dot_general(a, b, (((0,), (0,)), ((), ())), preferred_element_type=F32)


def _split_bf16(x):
    hi = x.astype(BF16)
    lo = (x - hi.astype(F32)).astype(BF16)
    return hi, lo


def _rms(x, g):
    return x * lax.rsqrt(jnp.mean(x * x, axis=-1, keepdims=True) + EPS) * g


def _softplus(z):
    return jnp.maximum(z, 0.0) + jnp.log(1.0 + jnp.exp(-jnp.abs(z)))


def _silu(x):
    return x * jax.nn.sigmoid(x)


def _rows(ref, rows):
    v = ref[...]
    nb, _, d = v.shape
    if nb == 1:
        return v[0]
    return jnp.broadcast_to(v, (nb, rows, d)).reshape(nb * rows, d)


class _Group:
    def __init__(self, n_seq, seq_len, mod_row0, tm_pref):
        self.n_seq, self.seq_len, self.mod_row0 = n_seq, seq_len, mod_row0
        self.n = n_seq * seq_len
        if seq_len >= tm_pref:
            assert seq_len % tm_pref == 0
            self.tm, self.nb = tm_pref, 1
        else:
            self.nb = min(n_seq, tm_pref // seq_len)
            assert n_seq % self.nb == 0
            self.tm = self.nb * seq_len
        self.rows = self.tm // self.nb
        self.blocks_per_seq = max(1, seq_len // self.tm)

    def mod_spec(self, layer, col, d):
        nb, bps, row0 = self.nb, self.blocks_per_seq, self.mod_row0

        def index(i, *_):
            if nb == 1:
                return (layer, row0 + i // bps, 0, col)
            return (layer, row0 // nb + i, 0, col)

        return pl.BlockSpec((None, nb, 1, d), index)


def _mod_kernel(c_ref, w_ref, b_ref, o_ref):
    cs = _silu(c_ref[...]).astype(BF16)
    o_ref[...] = _dot(cs, w_ref[...].astype(BF16)) + b_ref[...]


def _modulation(c_all, w_mod, b_mod):
    depth, d, n_out = w_mod.shape
    n_c = c_all.shape[0]
    tn = min(n_out, 1024)
    return pl.pallas_call(
        _mod_kernel,
        out_shape=jax.ShapeDtypeStruct((depth, n_c, n_out), F32),
        grid=(depth, n_out // tn),
        in_specs=[
            pl.BlockSpec((n_c, d), lambda l, j: (0, 0)),
            pl.BlockSpec((None, d, tn), lambda l, j: (l, 0, j)),
            pl.BlockSpec((None, 1, tn), lambda l, j: (l, 0, j)),
        ],
        out_specs=pl.BlockSpec((None, n_c, tn), lambda l, j: (l, 0, j)),
        compiler_params=_params(2),
        name="modulation",
    )(c_all, w_mod, b_mod.reshape(depth, 1, n_out))


def _inproj_kernel(*refs, seg_cols, seg_scales, rows, has_aux, n_carried, n_slabs):
    x_ref, g_ref, sh_ref, sc_ref, w_ref = refs[:5]
    pos = 5
    if has_aux:
        wah_ref, wal_ref = refs[5:7]
        pos = 7
    pos += n_carried
    out_refs = refs[pos:pos + len(seg_cols)]
    if has_aux:
        aux_ref = refs[pos + len(seg_cols)]
    tm = x_ref.shape[0]
    scale_rows, shift_rows = _rows(sc_ref, rows), _rows(sh_ref, rows)
    for s in range(n_slabs):
        rs = slice(s * tm // n_slabs, (s + 1) * tm // n_slabs)
        per_row = lambda v: v if v.shape[0] == 1 else v[rs]
        h = _rms(x_ref[rs, :], g_ref[...]) * (1.0 + per_row(scale_rows)) + per_row(shift_rows)
        h_hi, h_lo = _split_bf16(h)
        if has_aux:
            aux_ref[rs, :] = (_dot(h_hi, wah_ref[...]) + _dot(h_lo, wah_ref[...])
                              + _dot(h_hi, wal_ref[...]))
        for o_ref, (c0, c1), scale in zip(out_refs, seg_cols, seg_scales):
            y = _dot(h_hi, w_ref[:, c0:c1])
            if scale != 1.0:
                y = y * scale
            o_ref[rs, :] = y.astype(o_ref.dtype)


def _inproj(x, mod, layer, g, w, layer_i, segs, grp, aux_w=None):
    n, d = x.shape
    tm = grp.tm
    w_cols = w.shape[2]
    in_specs = [
        pl.BlockSpec((tm, d), lambda i: (i, 0)),
        pl.BlockSpec((1, d), lambda i: (0, 0)),
        grp.mod_spec(layer, 0, d),
        grp.mod_spec(layer, 1, d),
        pl.BlockSpec((None, d, w_cols), lambda i: (layer_i, 0, 0), pipeline_mode=pl.Buffered(1)),
    ]
    args = [x, g.reshape(1, d), mod, mod, w]
    has_aux = aux_w is not None
    if has_aux:
        wah, wal = aux_w
        wa = wah.shape[2]
        in_specs += [pl.BlockSpec((None, d, wa), lambda i: (layer_i, 0, 0))] * 2
        args += [wah, wal]
    out_shape, out_specs, aliases, seg_cols = [], [], {}, []
    c0 = 0
    for k, (wd, dt, _, stack) in enumerate(segs):
        seg_cols.append((c0, c0 + wd))
        c0 += wd
        if stack is None:
            out_shape.append(jax.ShapeDtypeStruct((n, wd), dt))
            out_specs.append(pl.BlockSpec((tm, wd), lambda i: (i, 0)))
        else:
            n_layers, previous = stack
            out_shape.append(jax.ShapeDtypeStruct((n_layers, n, wd), dt))
            out_specs.append(pl.BlockSpec((None, tm, wd), lambda i: (layer_i, i, 0)))
            if previous is not None:
                aliases[len(args)] = k
                in_specs.append(pl.BlockSpec(memory_space=pl.ANY))
                args.append(previous)
    assert c0 <= w_cols and w.shape[1] == d
    if has_aux:
        out_shape.append(jax.ShapeDtypeStruct((n, wa), F32))
        out_specs.append(pl.BlockSpec((tm, wa), lambda i: (i, 0)))
    return pl.pallas_call(
        functools.partial(_inproj_kernel, seg_cols=tuple(seg_cols),
                          seg_scales=tuple(sg[2] for sg in segs), rows=grp.rows, has_aux=has_aux,
                          n_carried=len(aliases), n_slabs=2 if tm % 32 == 0 else 1),
        out_shape=out_shape,
        grid=(n // tm,),
        in_specs=in_specs,
        out_specs=out_specs,
        input_output_aliases=aliases,
        compiler_params=_params(1),
        name="in_proj",
    )(*args)


def _outproj_kernel(*refs, n_a, rows):
    a_refs = refs[:n_a]
    w_refs = refs[n_a:2 * n_a]
    x_ref, g_ref, gt_ref, o_ref = refs[2 * n_a:]
    tm = x_ref.shape[0]
    n_slabs = 2 if tm % 32 == 0 else 1
    gate_rows = _rows(gt_ref, rows)
    for s in range(n_slabs):
        rs = slice(s * tm // n_slabs, (s + 1) * tm // n_slabs)
        y = _dot(a_refs[0][rs, :], w_refs[0][...])
        for a_ref, w_ref in zip(a_refs[1:], w_refs[1:]):
            y = y + _dot(a_ref[rs, :], w_ref[...])
        gate = gate_rows if gate_rows.shape[0] == 1 else gate_rows[rs]
        o_ref[rs, :] = x_ref[rs, :] + gate * _rms(y, g_ref[...])


def _outproj(acts, w, layer_i, x, mod, layer, g, grp):
    n, d = x.shape
    tm = grp.tm
    ka = acts[0].shape[1]
    assert all(a.shape[1] == ka for a in acts) and w.shape[1] == ka * len(acts)
    in_specs = [pl.BlockSpec((tm, ka), lambda i: (i, 0)) for _ in acts]
    in_specs += [pl.BlockSpec((None, ka, d), lambda i, r=r: (layer_i, r, 0)) for r in range(len(acts))]
    in_specs += [
        pl.BlockSpec((tm, d), lambda i: (i, 0)),
        pl.BlockSpec((1, d), lambda i: (0, 0)),
        grp.mod_spec(layer, 2, d),
    ]
    return pl.pallas_call(
        functools.partial(_outproj_kernel, n_a=len(acts), rows=grp.rows),
        out_shape=jax.ShapeDtypeStruct((n, d), F32),
        grid=(n // tm,),
        in_specs=in_specs,
        out_specs=pl.BlockSpec((tm, d), lambda i: (i, 0)),
        compiler_params=_params(1),
        name="out_proj",
    )(*acts, *([w] * len(acts)), x, g.reshape(1, d), mod)


def _ff_kernel(x_ref, g1_ref, sh_ref, sc_ref, wu_ref, wd_ref, g2_ref, gt_ref, o_ref,
               h_ref, t_ref, y_ref, *, rows, n_a, n_b):
    j = pl.program_id(1)
    tf = t_ref.shape[2]
    tn = y_ref.shape[2]

    def hidden(h):
        t = jnp.maximum(_dot(h, wu_ref[...]), 0.0)
        return (t * t).astype(BF16)

    @pl.when(j == 0)
    def _():
        tm = x_ref.shape[0]
        n_slabs = 2 if tm % 32 == 0 else 1
        scale_rows, shift_rows = _rows(sc_ref, rows), _rows(sh_ref, rows)
        for s in range(n_slabs):
            rs = slice(s * tm // n_slabs, (s + 1) * tm // n_slabs)
            per_row = lambda v: v if v.shape[0] == 1 else v[rs]
            h = _rms(x_ref[rs, :], g1_ref[...]) * (1.0 + per_row(scale_rows)) + per_row(shift_rows)
            h = h.astype(BF16)
            h_ref[rs, :] = h
            t_ref[0, rs, :] = hidden(h)

    @pl.when((j > 0) & (j < n_a))
    def _():
        t_ref[j] = hidden(h_ref[...])

    @pl.when(j >= n_a)
    def _():
        acc = _dot(t_ref[0], wd_ref[0:tf, :])
        for a in range(1, n_a):
            acc = acc + _dot(t_ref[a], wd_ref[a * tf:(a + 1) * tf, :])
        y_ref[j - n_a] = acc

    @pl.when(j == n_a + n_b - 1)
    def _():
        ss = jnp.sum(jnp.square(y_ref[0]), axis=-1, keepdims=True)
        for b in range(1, n_b):
            ss = ss + jnp.sum(jnp.square(y_ref[b]), axis=-1, keepdims=True)
        rs = lax.rsqrt(ss * (1.0 / (n_b * tn)) + EPS)
        gate = _rows(gt_ref, rows)
        g2 = g2_ref[...]
        for b in range(n_b):
            cs = slice(b * tn, (b + 1) * tn)
            o_ref[:, cs] = x_ref[:, cs] + gate[:, cs] * (y_ref[b] * rs * g2[:, cs])


def _ff(x, mod, layer, g1, g2, w_up, w_down, grp):
    n, d = x.shape
    f = w_up.shape[2]
    tm = grp.tm
    tf = min(1024, f)
    tn = min(2 * LANES, d)
    assert f % tf == 0 and d % tn == 0
    n_a, n_b = f // tf, d // tn
    return pl.pallas_call(
        functools.partial(_ff_kernel, rows=grp.rows, n_a=n_a, n_b=n_b),
        out_shape=jax.ShapeDtypeStruct((n, d), F32),
        grid=(n // tm, n_a + n_b),
        in_specs=[
            pl.BlockSpec((tm, d), lambda i, j: (i, 0)),
            pl.BlockSpec((1, d), lambda i, j: (0, 0)),
            grp.mod_spec(layer, 3, d),
            grp.mod_spec(layer, 4, d),
            pl.BlockSpec((None, d, tf), lambda i, j: (layer, 0, jnp.minimum(j, n_a - 1))),
            pl.BlockSpec((None, f, tn), lambda i, j: (layer, 0, jnp.clip(j - n_a, 0, n_b - 1))),
            pl.BlockSpec((1, d), lambda i, j: (0, 0)),
            grp.mod_spec(layer, 5, d),
        ],
        out_specs=pl.BlockSpec((tm, d), lambda i, j: (i, 0)),
        scratch_shapes=[pltpu.VMEM((tm, d), BF16), pltpu.VMEM((n_a, tm, tf), BF16),
                        pltpu.VMEM((n_b, tm, tn), F32)],
        compiler_params=_params(2),
        name="channel_mlp",
    )(x, g1.reshape(1, d), mod, mod, w_up, w_down, g2.reshape(1, d), mod)


def _strict_upper(n):
    r = lax.broadcasted_iota(jnp.int32, (n, n), 0)
    c = lax.broadcasted_iota(jnp.int32, (n, n), 1)
    return jnp.where(r > c, 1.0, 0.0).astype(BF16)


def _sb_prompt_kernel(q_ref, k_ref, v_ref, o_ref, kb_ref, vb_ref, lb_ref, hi_ref, lo_ref, *, t, dh):
    qi = pl.program_id(2)
    heads = range(q_ref.shape[1] // dh)
    cols = [slice(h * dh, (h + 1) * dh) for h in heads]

    @pl.when(qi == 0)
    def _():
        kb_ref[...] = k_ref[...].astype(BF16)
        vb_ref[...] = v_ref[...].astype(BF16)

    u = _strict_upper(t)
    qs = [q_ref[:, cs] for cs in cols]

    def logits(start):
        return [_dot_nt(q, kb_ref[pl.ds(start, t), cs]) for q, cs in zip(qs, cols)]

    def stage1(zs, slot, diagonal):
        sps = [_softplus(z) for z in zs]
        if diagonal:
            r = lax.broadcasted_iota(jnp.int32, (t, t), 0)
            c = lax.broadcasted_iota(jnp.int32, (t, t), 1)
            negs = [jnp.where(c < r, -sp, 0.0) for sp in sps]
            log_betas = [jnp.where(c < r, z - sp, MASKED_LOG) for z, sp in zip(zs, sps)]
        else:
            negs = [-sp for sp in sps]
            log_betas = [z - sp for z, sp in zip(zs, sps)]
        for h in heads:
            lb_ref[slot, h] = log_betas[h]
            hi_ref[slot, h], lo_ref[slot, h] = _split_bf16(negs[h])
        return tuple(jnp.sum(neg, axis=-1, keepdims=True) for neg in negs)

    def suffixes(slot):
        return [_dot(hi_ref[slot, h], u) + _dot(lo_ref[slot, h], u) for h in heads]

    def stage2(sufs, start, slot, carries, accs):
        ws = [jnp.exp(lb_ref[slot, h] + sufs[h] + carries[h]).astype(BF16) for h in heads]
        return tuple(accs[h] + _dot(ws[h], vb_ref[pl.ds(start, t), cols[h]]) for h in heads)

    sums = stage1(logits(pl.multiple_of(qi * t, t)), 0, True)
    carries = (jnp.zeros((t, 1), F32),) * len(heads)
    accs = (jnp.zeros((t, dh), F32),) * len(heads)

    def body(it, state):
        carries, accs, sums = state
        slot = it % 2
        sufs = suffixes(slot)
        zs = logits(pl.multiple_of((qi - 1 - it) * t, t))
        accs = stage2(sufs, pl.multiple_of((qi - it) * t, t), slot, carries, accs)
        new_sums = stage1(zs, 1 - slot, False)
        carries = tuple(c + s for c, s in zip(carries, sums))
        return carries, accs, new_sums

    carries, accs, _ = lax.fori_loop(0, qi, body, (carries, accs, sums))
    accs = stage2(suffixes(qi % 2), 0, qi % 2, carries, accs)
    for h in heads:
        o_ref[:, cols[h]] = accs[h].astype(o_ref.dtype)


def _sb_prompt(q, k, v, layer_i, n_heads, t_pref=256, heads_per_step=4):
    b, l, w = q.shape
    dh = w // n_heads
    t = min(t_pref, l)
    hs = min(heads_per_step, n_heads)
    assert l % t == 0 and n_heads % hs == 0
    q_spec = pl.BlockSpec((None, t, hs * dh), lambda bi, h, qi: (bi, qi, h))
    kv_spec = pl.BlockSpec((None, None, l, hs * dh), lambda bi, h, qi: (layer_i, bi, 0, h))
    return pl.pallas_call(
        functools.partial(_sb_prompt_kernel, t=t, dh=dh),
        out_shape=jax.ShapeDtypeStruct((b, l, w), BF16),
        grid=(b, n_heads // hs, l // t),
        in_specs=[q_spec, kv_spec, kv_spec],
        out_specs=q_spec,
        scratch_shapes=[pltpu.VMEM((l, hs * dh), BF16), pltpu.VMEM((l, hs * dh), BF16),
                        pltpu.VMEM((2, hs, t, t), F32), pltpu.VMEM((2, hs, t, t), BF16),
                        pltpu.VMEM((2, hs, t, t), BF16)],
        compiler_params=_params(3),
        name="stick_breaking_prompt",
    )(q, k, v)


def _sb_sample_kernel(q_ref, kn_ref, vn_ref, kc_ref, vc_ref, o_ref, *, tk, n_heads):
    lq, w = q_ref.shape
    dh = kc_ref.shape[1]
    past = kc_ref.shape[0] // n_heads
    nl = n_heads * lq

    qf = q_ref[...].astype(F32)
    row_h = lax.broadcasted_iota(jnp.int32, (nl, w), 0) // lq
    col_h = lax.broadcasted_iota(jnp.int32, (nl, w), 1) // dh
    q_exp = jnp.where(row_h == col_h, jnp.concatenate([qf] * n_heads, axis=0), 0.0).astype(BF16)

    def strict_lower_t(n):
        r = lax.broadcasted_iota(jnp.int32, (n, n), 0)
        c = lax.broadcasted_iota(jnp.int32, (n, n), 1)
        return jnp.where(c > r, 1.0, 0.0).astype(BF16)

    def block(k_heads, v_heads, carry, accs, new):
        n = k_heads[0].shape[0]
        z = _dot_nt(k_heads[0], q_exp[:, 0:dh])
        for h in range(1, n_heads):
            z = z + _dot_nt(k_heads[h], q_exp[:, h * dh:(h + 1) * dh])
        sp = _softplus(z)
        if new:
            key = lax.broadcasted_iota(jnp.int32, (n, nl), 0)
            qry = lax.broadcasted_iota(jnp.int32, (n, nl), 1) % lq
            mask = key < qry
            neg = jnp.where(mask, -sp, 0.0)
        else:
            neg = -sp
        hi, lo = _split_bf16(neg)
        ut = strict_lower_t(n)
        suffix = _dot(ut, hi) + _dot(ut, lo)
        wgt = jnp.exp(z - sp + suffix + carry)
        if new:
            wgt = jnp.where(mask, wgt, 0.0)
        wt = wgt.T.astype(BF16)
        accs = [acc + _dot(wt[h * lq:(h + 1) * lq, :], v_heads[h]) for h, acc in enumerate(accs)]
        carry = carry + jnp.sum(neg, axis=0, keepdims=True)
        return carry, accs

    pad = jnp.zeros((LANES - lq, w), F32)
    kn = jnp.concatenate([kn_ref[...], pad], axis=0).astype(BF16)
    vn = jnp.concatenate([vn_ref[...], pad], axis=0).astype(BF16)
    heads = lambda a: [a[:, h * dh:(h + 1) * dh] for h in range(n_heads)]
    carry, accs = block(heads(kn), heads(vn), jnp.zeros((1, nl), F32),
                        [jnp.zeros((lq, dh), F32)] * n_heads, True)
    for i in reversed(range(past // tk)):
        rows = [pl.ds(i * tk * n_heads + h, tk, stride=n_heads) for h in range(n_heads)]
        ks = [kc_ref[r, :].astype(BF16) for r in rows]
        vs = [vc_ref[r, :].astype(BF16) for r in rows]
        carry, accs = block(ks, vs, carry, accs, False)

    o_ref[...] = jnp.concatenate(accs, axis=1).astype(o_ref.dtype)


def _sb_sample(q, k_new, v_new, cache_k, cache_v, layer_i, tk_pref=256):
    b, lq, w = q.shape
    _, _, past, n_heads, dh = cache_k.shape
    tk = min(tk_pref, past)
    assert past % tk == 0 and lq <= LANES
    new_spec = pl.BlockSpec((None, lq, w), lambda bi: (bi, 0, 0))
    kv_new_spec = pl.BlockSpec((None, None, lq, w), lambda bi: (layer_i, bi, 0, 0))
    cache_spec = pl.BlockSpec((None, None, past * n_heads, dh), lambda bi: (layer_i, bi, 0, 0))
    flat = lambda c: c.reshape(c.shape[0], c.shape[1], past * n_heads, dh)
    return pl.pallas_call(
        functools.partial(_sb_sample_kernel, tk=tk, n_heads=n_heads),
        out_shape=jax.ShapeDtypeStruct((b, lq, w), BF16),
        grid=(b,),
        in_specs=[new_spec, kv_new_spec, kv_new_spec, cache_spec, cache_spec],
        out_specs=new_spec,
        compiler_params=_params(1),
        name="stick_breaking_sample",
    )(q, k_new, v_new, flat(cache_k), flat(cache_v))


def _sgu_kernel(u_ref, gv_ref, g_ref, w_ref, b_ref, o_ref, *rest, r, gd):
    vb_ref = rest[-1]
    rb, width = u_ref.shape
    vn = _rms(jax.nn.gelu(gv_ref[...]), g_ref[...])
    if len(rest) == 2:
        rest[0][...] = vn
    vb_ref[...] = vn.astype(BF16)

    def chunk(s, _):
        rs = pl.ds(pl.multiple_of(s * r, r), r)
        for gi in range(width // gd):
            cs = slice(gi * gd, (gi + 1) * gd)
            mix = _dot(w_ref[gi], vb_ref[rs, cs]) + b_ref[:, cs]
            o_ref[rs, cs] = (jax.nn.gelu(u_ref[rs, cs].astype(F32)) * mix).astype(o_ref.dtype)
        return 0

    lax.fori_loop(0, rb // r, chunk, 0)


def _sgu(u, gv, g_sgu, w_mix, bias, layer_i, emit_vn, rb_pref=512):
    n, width = u.shape
    _, groups, r, _ = w_mix.shape
    rb = max(r, min(rb_pref, n))
    assert n % rb == 0 and rb % r == 0
    row_spec = pl.BlockSpec((rb, width), lambda i: (i, 0))
    out_shape = [jax.ShapeDtypeStruct((n, width), BF16)]
    out_specs = [row_spec]
    if emit_vn:
        out_shape.append(jax.ShapeDtypeStruct((n, width), F32))
        out_specs.append(row_spec)
    res = pl.pallas_call(
        functools.partial(_sgu_kernel, r=r, gd=width // groups),
        out_shape=out_shape,
        grid=(n // rb,),
        in_specs=[
            row_spec, row_spec,
            pl.BlockSpec((1, width), lambda i: (0, 0)),
            pl.BlockSpec((None, groups, r, r), lambda i: (layer_i, 0, 0, 0)),
            pl.BlockSpec((None, r, width), lambda i: (layer_i, 0, 0)),
        ],
        out_specs=out_specs,
        scratch_shapes=[pltpu.VMEM((rb, width), BF16)],
        compiler_params=_params(1),
        name="spatial_gating",
    )(u, gv, g_sgu.reshape(1, width), w_mix, bias)
    return res if emit_vn else (res[0], None)


def _gla_kernel(*refs, lc, has_state, has_carried, q_scale):
    q_ref, k_ref, v_ref, r_ref, a_ref, wgh_ref, wgl_ref, bg_ref, gg_ref = refs[:9]
    s0_ref = refs[9] if has_state else None
    o_ref, s_ref, st_ref, lg_ref = refs[9 + has_state + has_carried:]
    tb = pl.program_id(2)
    n_rows, dk = q_ref.shape
    n_chunks = n_rows // lc

    @pl.when(tb == 0)
    def _():
        if has_state:
            st_ref[...] = s0_ref[...].T
        else:
            st_ref[...] = jnp.zeros_like(st_ref)

    a_hi, a_lo = _split_bf16(a_ref[...])
    pre = (_dot(a_hi, wgh_ref[...]) + _dot(a_lo, wgh_ref[...]) + _dot(a_hi, wgl_ref[...])
           + bg_ref[...])
    lg_ref[...] = -_softplus(-pre) * (1.0 / GLA_TAU)

    row = lax.broadcasted_iota(jnp.int32, (lc, lc), 0)
    col = lax.broadcasted_iota(jnp.int32, (lc, lc), 1)
    lower = jnp.where(col <= row, 1.0, 0.0).astype(BF16)

    chunks = [slice(c * lc, (c + 1) * lc) for c in range(n_chunks)]
    per_chunk = lambda x: [x[cs] for cs in chunks]

    def roll_in_group(x, d):
        x3 = x.reshape(n_rows // SUBLANES, SUBLANES, x.shape[1])
        return pltpu.roll(x3, d, 1).reshape(x.shape)

    lg_hi, lg_lo = _split_bf16(lg_ref[...])
    b = jnp.concatenate(
        [_dot(lower, hi) + _dot(lower, lo) for hi, lo in zip(per_chunk(lg_hi), per_chunk(lg_lo))],
        axis=0)
    q = q_ref[...].astype(F32) * q_scale
    k = k_ref[...].astype(F32)

    scores = [jnp.zeros((lc, lc), F32)] * n_chunks
    h = lc // 2
    while h >= SUBLANES:
        b3 = b.reshape(n_rows // (2 * h), 2 * h, dk)
        f = jnp.exp(-jnp.abs(b3 - b3[:, h - 1:h, :])).reshape(n_rows, dk)
        qf, kf = (q * f).astype(BF16), (k * f).astype(BF16)
        mask = ((row // h) == (col // h) + 1) & ((col // h) % 2 == 0)
        scores = [sc + jnp.where(mask, _dot_nt(qc, kc), 0.0)
                  for sc, qc, kc in zip(scores, per_chunk(qf), per_chunk(kf))]
        h //= 2
    for d in range(SUBLANES):
        if d == 0:
            pd = jnp.sum(q * k, axis=-1, keepdims=True)
        else:
            kd = roll_in_group(k, d)
            bd = roll_in_group(b, d)
            pd = jnp.sum(q * kd * jnp.exp(jnp.minimum(b - bd, 0.0)), axis=-1, keepdims=True)
        mask = (col == row - d) & ((row % SUBLANES) >= d)
        scores = [sc + jnp.where(mask, pc, 0.0) for sc, pc in zip(scores, per_chunk(pd))]

    b_last = jnp.broadcast_to(b.reshape(n_chunks, lc, dk)[:, lc - 1:lc, :], (n_chunks, lc, dk))
    decay = [jnp.exp(bl[0:1, :]) for bl in per_chunk(b_last.reshape(n_rows, dk))]
    qd = per_chunk((q * jnp.exp(b)).astype(BF16))
    k_dec = per_chunk((k * jnp.exp(b_last.reshape(n_rows, dk) - b)).astype(BF16))
    vs = [v_ref[cs, :] for cs in chunks]
    intra = [_dot(sc.astype(BF16), vc) for sc, vc in zip(scores, vs)]
    kv = [_dot_tn(vc, kc) for vc, kc in zip(vs, k_dec)]

    st = st_ref[...]
    o_inter = _dot_nt(qd[0], st.astype(BF16))
    for c, cs in enumerate(chunks):
        o = o_inter + intra[c]
        st = st * decay[c] + kv[c]
        if c + 1 < n_chunks:
            o_inter = _dot_nt(qd[c + 1], st.astype(BF16))
        on = _rms(o, gg_ref[...])
        o_ref[cs, :] = (on * _silu(r_ref[cs, :].astype(F32))).astype(o_ref.dtype)
    st_ref[...] = st

    @pl.when(tb == pl.num_programs(2) - 1)
    def _():
        s_ref[...] = st_ref[...].T


def _gla(q, k, v, r, a, wg_hi, wg_lo, b_gate, g_gla, state, layer_i, n_heads, states_out,
         tb_pref=512):
    b, l, wq = q.shape
    dk = wq // n_heads
    dv = v.shape[2] // n_heads
    ra = a.shape[2]
    lc = min(GLA_CHUNK, l)
    tb = min(tb_pref, l)
    assert l % tb == 0 and tb % lc == 0 and lc % (2 * SUBLANES) == 0
    qk_spec = pl.BlockSpec((None, tb, dk), lambda bi, h, t: (bi, t, h))
    vr_spec = pl.BlockSpec((None, tb, dv), lambda bi, h, t: (bi, t, h))
    in_specs = [
        qk_spec, qk_spec, vr_spec, vr_spec,
        pl.BlockSpec((None, tb, ra), lambda bi, h, t: (bi, t, 0)),
        pl.BlockSpec((None, ra, dk), lambda bi, h, t: (layer_i, 0, h)),
        pl.BlockSpec((None, ra, dk), lambda bi, h, t: (layer_i, 0, h)),
        pl.BlockSpec((None, 1, dk), lambda bi, h, t: (layer_i, 0, h)),
        pl.BlockSpec((None, 1, dv), lambda bi, h, t: (layer_i, 0, 0)),
    ]
    args = [q, k, v, r, a, wg_hi, wg_lo, b_gate.reshape(-1, 1, wq), g_gla.reshape(-1, 1, dv)]
    has_state = state is not None
    if has_state:
        in_specs.append(
            pl.BlockSpec((None, None, None, dk, dv), lambda bi, h, t: (layer_i, bi, h, 0, 0)))
        args.append(state)
    aliases = {}
    if states_out is not None:
        aliases[len(args)] = 1
        in_specs.append(pl.BlockSpec(memory_space=pl.ANY))
        args.append(states_out)
    n_layers = b_gate.shape[0]
    return pl.pallas_call(
        functools.partial(_gla_kernel, lc=lc, has_state=has_state,
                          has_carried=states_out is not None, q_scale=dk ** -0.5),
        out_shape=[jax.ShapeDtypeStruct((b, l, n_heads * dv), BF16),
                   jax.ShapeDtypeStruct((n_layers, b, n_heads, dk, dv), F32)],
        grid=(b, n_heads, l // tb),
        in_specs=in_specs,
        out_specs=[vr_spec, pl.BlockSpec((None, None, None, dk, dv),
                                         lambda bi, h, t: (layer_i, bi, h, 0, 0))],
        scratch_shapes=[pltpu.VMEM((dv, dk), F32), pltpu.VMEM((tb, dk), F32)],
        input_output_aliases=aliases,
        compiler_params=_params(3),
        name="gated_linear_attention",
    )(*args)


def _trunk(x3, grp, mod, weights, dims, cache_k, cache_v, state_gla):
    (g_norm, w_in_even, w_out_even, g_sgu, sgu_mix, sgu_bias, w_in_odd, w_a, wg, b_gate, g_gla,
     w_out_odd, w_ff_up, w_ff_down) = weights
    sb_heads, sb_width, sgu_width, gla_heads, gla_qk, gla_vw = dims
    n_seq, seq_len, d = x3.shape
    sample = cache_k is not None
    x = x3.reshape(n_seq * seq_len, d)
    seq = lambda t: t.reshape(n_seq, seq_len, t.shape[-1])
    depth = g_norm.shape[0]
    n_even = (depth + 1) // 2
    k_all = v_all = states = None
    gvs = []
    for layer in range(depth):
        i = layer // 2
        if layer % 2 == 0:
            sb_scale = (sb_width // sb_heads) ** -0.5
            q, k_all, v_all, u, gv = _inproj(
                x, mod, layer, g_norm[layer, 0], w_in_even, i,
                [(sb_width, BF16, sb_scale, None), (sb_width, F32, 1.0, (n_even, k_all)),
                 (sb_width, F32, 1.0, (n_even, v_all)), (sgu_width, BF16, 1.0, None),
                 (sgu_width, F32, 1.0, None)], grp)
            kv_seq = lambda t: t.reshape(n_even, n_seq, seq_len, sb_width)
            if sample:
                o_a = _sb_sample(seq(q), kv_seq(k_all), kv_seq(v_all), cache_k, cache_v, i)
            else:
                o_a = _sb_prompt(seq(q), kv_seq(k_all), kv_seq(v_all), i, sb_heads)
            o_b, vn = _sgu(u, gv, g_sgu[i], sgu_mix, sgu_bias, i, emit_vn=sample)
            x = _outproj([o_a.reshape(x.shape[0], sb_width), o_b], w_out_even, i, x, mod, layer,
                         g_norm[layer, 1], grp)
            gvs.append(vn)
        else:
            q, k, v, r, a = _inproj(
                x, mod, layer, g_norm[layer, 0], w_in_odd, i,
                [(gla_qk, BF16, 1.0, None), (gla_qk, BF16, 1.0, None), (gla_vw, BF16, 1.0, None),
                 (gla_vw, BF16, 1.0, None)], grp, aux_w=w_a)
            o, states = _gla(seq(q), seq(k), seq(v), seq(r), seq(a), wg[0], wg[1], b_gate, g_gla,
                             state_gla if sample else None, i, gla_heads, states)
            x = _outproj([o.reshape(x.shape[0], gla_vw)], w_out_odd, i, x, mod, layer,
                         g_norm[layer, 1], grp)
        x = _ff(x, mod, layer, g_norm[layer, 2], g_norm[layer, 3], w_ff_up, w_ff_down, grp)
    return x.reshape(n_seq, seq_len, d), k_all, v_all, states, gvs


def kernel(x_prompt, x_sample, cache_sb_k, cache_sb_v, state_gla, c_prompt, c_sample, w_mod, b_mod, g_norm, w_in_even, w_out_even, g_sgu, w_sgu, b_sgu, w_in_odd, w_gate_up, b_gate, g_gla, w_out_odd, w_ff_up, w_ff_down):
    batch, seq, d = x_prompt.shape
    dec_batch, dec_seq, _ = x_sample.shape
    n_even, _, past, sb_heads, sb_dh = cache_sb_k.shape
    sb_width = sb_heads * sb_dh
    _, _, gla_heads, gla_dk, gla_dv = state_gla.shape
    gla_qk, gla_vw = gla_heads * gla_dk, gla_heads * gla_dv
    _, groups, sgu_chunk, _ = w_sgu.shape
    sgu_width = g_sgu.shape[1]
    rank = w_gate_up.shape[1]
    depth, d_ff = w_ff_up.shape[0], w_ff_up.shape[2]
    dims = (sb_heads, sb_width, sgu_width, gla_heads, gla_qk, gla_vw)

    mod = _modulation(jnp.concatenate([c_sample, c_prompt], axis=0), w_mod, b_mod)
    mod = mod.reshape(mod.shape[0], mod.shape[1], 1, mod.shape[2])

    cast = lambda t: t.astype(BF16)
    n_main = 2 * gla_qk + 2 * gla_vw
    w_a = jnp.pad(w_in_odd[:, :, n_main:], ((0, 0), (0, 0), (0, LANES - rank)))
    w_a_hi = cast(w_a)
    w_a = (w_a_hi, cast(w_a - w_a_hi.astype(F32)))
    wg = jnp.pad(w_gate_up, ((0, 0), (0, LANES - rank), (0, 0)))
    wg_hi = cast(wg)
    wg = (wg_hi, cast(wg - wg_hi.astype(F32)))
    w_tri = w_sgu * jnp.tril(jnp.ones((sgu_chunk, sgu_chunk), F32))

    def sgu_tables(chunk_len):
        mix = cast(w_tri[:, :, :chunk_len, :chunk_len])
        bias = jnp.repeat(jnp.swapaxes(b_sgu[:, :, :chunk_len], 1, 2), sgu_width // groups, axis=2)
        return mix, bias

    common = (g_norm, cast(w_in_even), cast(w_out_even), g_sgu)
    tail = (cast(w_in_odd), w_a, wg, b_gate, g_gla, cast(w_out_odd), cast(w_ff_up), cast(w_ff_down))

    grp_p = _Group(batch, seq, dec_batch, 512)
    grp_s = _Group(dec_batch, dec_seq, 0, 512)

    y_p, ks_p, vs_p, st_p, _ = _trunk(
        x_prompt, grp_p, mod, common + sgu_tables(sgu_chunk) + tail, dims, None, None, None)
    y_s, ks_s, vs_s, st_s, gv_s = _trunk(
        x_sample, grp_s, mod, common + sgu_tables(dec_seq) + tail, dims,
        cache_sb_k, cache_sb_v, state_gla)

    heads = lambda t, b, l: t.reshape(t.shape[0], b, l, sb_heads, sb_dh)
    return (y_p, y_s,
            heads(ks_p, batch, seq), heads(vs_p, batch, seq), st_p,
            heads(ks_s, dec_batch, dec_seq), heads(vs_s, dec_batch, dec_seq), st_s,
            jnp.stack(gv_s).reshape(len(gv_s), dec_batch, dec_seq, sgu_width))
```

```python
import functools

import jax
import jax.numpy as jnp
from jax import lax
from jax.experimental import pallas as pl
from jax.experimental.pallas import tpu as pltpu

F32 = jnp.float32
BF16 = jnp.bfloat16

EPS = 1e-6
GLA_CHUNK = 64
GLA_TAU = 16.0
SUBLANES = 8
LANES = 128
MASKED_LOG = -1e30
VMEM_LIMIT_BYTES = 56 * 1024 * 1024


def _params(n_grid):
    return pltpu.CompilerParams(
        dimension_semantics=("arbitrary",) * n_grid,
        vmem_limit_bytes=VMEM_LIMIT_BYTES)


def _dot(a, b):
    return jnp.dot(a, b, preferred_element_type=F32)


def _dot_nt(a, b):
    return lax.dot_general(a, b, (((1,), (1,)), ((), ())), preferred_element_type=F32)


def _dot_tn(a, b):
    return lax.dot_general(a, b, (((0,), (0,)), ((), ())), preferred_element_type=F32)


def _split_bf16(x):
    hi = x.astype(BF16)
    lo = (x - hi.astype(F32)).astype(BF16)
    return hi, lo


def _rms(x, g):
    return x * lax.rsqrt(jnp.mean(x * x, axis=-1, keepdims=True) + EPS) * g


def _softplus(z):
    return jnp.maximum(z, 0.0) + jnp.log(1.0 + jnp.exp(-jnp.abs(z)))


def _silu(x):
    return x * jax.nn.sigmoid(x)


def _rows(ref, rows):
    v = ref[...]
    nb, _, d = v.shape
    if nb == 1:
        return v[0]
    return jnp.broadcast_to(v, (nb, rows, d)).reshape(nb * rows, d)


class _Group:
    def __init__(self, n_seq, seq_len, mod_row0, tm_pref):
        self.n_seq, self.seq_len, self.mod_row0 = n_seq, seq_len, mod_row0
        self.n = n_seq * seq_len
        if seq_len >= tm_pref:
            assert seq_len % tm_pref == 0
            self.tm, self.nb = tm_pref, 1
        else:
            self.nb = min(n_seq, tm_pref // seq_len)
            assert n_seq % self.nb == 0
            self.tm = self.nb * seq_len
        self.rows = self.tm // self.nb
        self.blocks_per_seq = max(1, seq_len // self.tm)

    def mod_spec(self, layer, col, d):
        nb, bps, row0 = self.nb, self.blocks_per_seq, self.mod_row0

        def index(i, *_):
            if nb == 1:
                return (layer, row0 + i // bps, 0, col)
            return (layer, row0 // nb + i, 0, col)

        return pl.BlockSpec((None, nb, 1, d), index)


def _mod_kernel(c_ref, w_ref, b_ref, o_ref):
    cs = _silu(c_ref[...]).astype(BF16)
    o_ref[...] = _dot(cs, w_ref[...].astype(BF16)) + b_ref[...]


def _modulation(c_all, w_mod, b_mod):
    depth, d, n_out = w_mod.shape
    n_c = c_all.shape[0]
    tn = min(n_out, 1024)
    return pl.pallas_call(
        _mod_kernel,
        out_shape=jax.ShapeDtypeStruct((depth, n_c, n_out), F32),
        grid=(depth, n_out // tn),
        in_specs=[
            pl.BlockSpec((n_c, d), lambda l, j: (0, 0)),
            pl.BlockSpec((None, d, tn), lambda l, j: (l, 0, j)),
            pl.BlockSpec((None, 1, tn), lambda l, j: (l, 0, j)),
        ],
        out_specs=pl.BlockSpec((None, n_c, tn), lambda l, j: (l, 0, j)),
        compiler_params=_params(2),
        name="modulation",
    )(c_all, w_mod, b_mod.reshape(depth, 1, n_out))


def _inproj_kernel(*refs, seg_cols, seg_scales, rows, has_aux, n_carried, n_slabs):
    x_ref, g_ref, sh_ref, sc_ref, w_ref = refs[:5]
    pos = 5
    if has_aux:
        wah_ref, wal_ref = refs[5:7]
        pos = 7
    pos += n_carried
    out_refs = refs[pos:pos + len(seg_cols)]
    if has_aux:
        aux_ref = refs[pos + len(seg_cols)]
    tm = x_ref.shape[0]
    scale_rows, shift_rows = _rows(sc_ref, rows), _rows(sh_ref, rows)
    for s in range(n_slabs):
        rs = slice(s * tm // n_slabs, (s + 1) * tm // n_slabs)
        per_row = lambda v: v if v.shape[0] == 1 else v[rs]
        h = _rms(x_ref[rs, :], g_ref[...]) * (1.0 + per_row(scale_rows)) + per_row(shift_rows)
        h_hi, h_lo = _split_bf16(h)
        if has_aux:
            aux_ref[rs, :] = (_dot(h_hi, wah_ref[...]) + _dot(h_lo, wah_ref[...])
                              + _dot(h_hi, wal_ref[...]))
        for o_ref, (c0, c1), scale in zip(out_refs, seg_cols, seg_scales):
            y = _dot(h_hi, w_ref[:, c0:c1])
            if scale != 1.0:
                y = y * scale
            o_ref[rs, :] = y.astype(o_ref.dtype)


def _inproj(x, mod, layer, g, w, layer_i, segs, grp, aux_w=None):
    n, d = x.shape
    tm = grp.tm
    w_cols = w.shape[2]
    in_specs = [
        pl.BlockSpec((tm, d), lambda i: (i, 0)),
        pl.BlockSpec((1, d), lambda i: (0, 0)),
        grp.mod_spec(layer, 0, d),
        grp.mod_spec(layer, 1, d),
        pl.BlockSpec((None, d, w_cols), lambda i: (layer_i, 0, 0), pipeline_mode=pl.Buffered(1)),
    ]
    args = [x, g.reshape(1, d), mod, mod, w]
    has_aux = aux_w is not None
    if has_aux:
        wah, wal = aux_w
        wa = wah.shape[2]
        in_specs += [pl.BlockSpec((None, d, wa), lambda i: (layer_i, 0, 0))] * 2
        args += [wah, wal]
    out_shape, out_specs, aliases, seg_cols = [], [], {}, []
    c0 = 0
    for k, (wd, dt, _, stack) in enumerate(segs):
        seg_cols.append((c0, c0 + wd))
        c0 += wd
        if stack is None:
            out_shape.append(jax.ShapeDtypeStruct((n, wd), dt))
            out_specs.append(pl.BlockSpec((tm, wd), lambda i: (i, 0)))
        else:
            n_layers, previous = stack
            out_shape.append(jax.ShapeDtypeStruct((n_layers, n, wd), dt))
            out_specs.append(pl.BlockSpec((None, tm, wd), lambda i: (layer_i, i, 0)))
            if previous is not None:
                aliases[len(args)] = k
                in_specs.append(pl.BlockSpec(memory_space=pl.ANY))
                args.append(previous)
    assert c0 <= w_cols and w.shape[1] == d
    if has_aux:
        out_shape.append(jax.ShapeDtypeStruct((n, wa), F32))
        out_specs.append(pl.BlockSpec((tm, wa), lambda i: (i, 0)))
    return pl.pallas_call(
        functools.partial(_inproj_kernel, seg_cols=tuple(seg_cols),
                          seg_scales=tuple(sg[2] for sg in segs), rows=grp.rows, has_aux=has_aux,
                          n_carried=len(aliases), n_slabs=2 if tm % 32 == 0 else 1),
        out_shape=out_shape,
        grid=(n // tm,),
        in_specs=in_specs,
        out_specs=out_specs,
        input_output_aliases=aliases,
        compiler_params=_params(1),
        name="in_proj",
    )(*args)


def _outproj_kernel(*refs, n_a, rows):
    a_refs = refs[:n_a]
    w_refs = refs[n_a:2 * n_a]
    x_ref, g_ref, gt_ref, o_ref = refs[2 * n_a:]
    tm = x_ref.shape[0]
    n_slabs = 2 if tm % 32 == 0 else 1
    gate_rows = _rows(gt_ref, rows)
    for s in range(n_slabs):
        rs = slice(s * tm // n_slabs, (s + 1) * tm // n_slabs)
        y = _dot(a_refs[0][rs, :], w_refs[0][...])
        for a_ref, w_ref in zip(a_refs[1:], w_refs[1:]):
            y = y + _dot(a_ref[rs, :], w_ref[...])
        gate = gate_rows if gate_rows.shape[0] == 1 else gate_rows[rs]
        o_ref[rs, :] = x_ref[rs, :] + gate * _rms(y, g_ref[...])


def _outproj(acts, w, layer_i, x, mod, layer, g, grp):
    n, d = x.shape
    tm = grp.tm
    ka = acts[0].shape[1]
    assert all(a.shape[1] == ka for a in acts) and w.shape[1] == ka * len(acts)
    in_specs = [pl.BlockSpec((tm, ka), lambda i: (i, 0)) for _ in acts]
    in_specs += [pl.BlockSpec((None, ka, d), lambda i, r=r: (layer_i, r, 0)) for r in range(len(acts))]
    in_specs += [
        pl.BlockSpec((tm, d), lambda i: (i, 0)),
        pl.BlockSpec((1, d), lambda i: (0, 0)),
        grp.mod_spec(layer, 2, d),
    ]
    return pl.pallas_call(
        functools.partial(_outproj_kernel, n_a=len(acts), rows=grp.rows),
        out_shape=jax.ShapeDtypeStruct((n, d), F32),
        grid=(n // tm,),
        in_specs=in_specs,
        out_specs=pl.BlockSpec((tm, d), lambda i: (i, 0)),
        compiler_params=_params(1),
        name="out_proj",
    )(*acts, *([w] * len(acts)), x, g.reshape(1, d), mod)


def _ff_kernel(x_ref, g1_ref, sh_ref, sc_ref, wu_ref, wd_ref, g2_ref, gt_ref, o_ref,
               h_ref, t_ref, y_ref, *, rows, n_a, n_b):
    j = pl.program_id(1)
    tf = t_ref.shape[2]
    tn = y_ref.shape[2]

    def hidden(h):
        t = jnp.maximum(_dot(h, wu_ref[...]), 0.0)
        return (t * t).astype(BF16)

    @pl.when(j == 0)
    def _():
        tm = x_ref.shape[0]
        n_slabs = 2 if tm % 32 == 0 else 1
        scale_rows, shift_rows = _rows(sc_ref, rows), _rows(sh_ref, rows)
        for s in range(n_slabs):
            rs = slice(s * tm // n_slabs, (s + 1) * tm // n_slabs)
            per_row = lambda v: v if v.shape[0] == 1 else v[rs]
            h = _rms(x_ref[rs, :], g1_ref[...]) * (1.0 + per_row(scale_rows)) + per_row(shift_rows)
            h = h.astype(BF16)
            h_ref[rs, :] = h
            t_ref[0, rs, :] = hidden(h)

    @pl.when((j > 0) & (j < n_a))
    def _():
        t_ref[j] = hidden(h_ref[...])

    @pl.when(j >= n_a)
    def _():
        acc = _dot(t_ref[0], wd_ref[0:tf, :])
        for a in range(1, n_a):
            acc = acc + _dot(t_ref[a], wd_ref[a * tf:(a + 1) * tf, :])
        y_ref[j - n_a] = acc

    @pl.when(j == n_a + n_b - 1)
    def _():
        ss = jnp.sum(jnp.square(y_ref[0]), axis=-1, keepdims=True)
        for b in range(1, n_b):
            ss = ss + jnp.sum(jnp.square(y_ref[b]), axis=-1, keepdims=True)
        rs = lax.rsqrt(ss * (1.0 / (n_b * tn)) + EPS)
        gate = _rows(gt_ref, rows)
        g2 = g2_ref[...]
        for b in range(n_b):
            cs = slice(b * tn, (b + 1) * tn)
            o_ref[:, cs] = x_ref[:, cs] + gate[:, cs] * (y_ref[b] * rs * g2[:, cs])


def _ff(x, mod, layer, g1, g2, w_up, w_down, grp):
    n, d = x.shape
    f = w_up.shape[2]
    tm = grp.tm
    tf = min(2048, f)
    tn = min(2 * LANES, d)
    assert f % tf == 0 and d % tn == 0
    n_a, n_b = f // tf, d // tn
    return pl.pallas_call(
        functools.partial(_ff_kernel, rows=grp.rows, n_a=n_a, n_b=n_b),
        out_shape=jax.ShapeDtypeStruct((n, d), F32),
        grid=(n // tm, n_a + n_b),
        in_specs=[
            pl.BlockSpec((tm, d), lambda i, j: (i, 0)),
            pl.BlockSpec((1, d), lambda i, j: (0, 0)),
            grp.mod_spec(layer, 3, d),
            grp.mod_spec(layer, 4, d),
            pl.BlockSpec((None, d, tf), lambda i, j: (layer, 0, jnp.minimum(j, n_a - 1))),
            pl.BlockSpec((None, f, tn), lambda i, j: (layer, 0, jnp.clip(j - n_a, 0, n_b - 1))),
            pl.BlockSpec((1, d), lambda i, j: (0, 0)),
            grp.mod_spec(layer, 5, d),
        ],
        out_specs=pl.BlockSpec((tm, d), lambda i, j: (i, 0)),
        scratch_shapes=[pltpu.VMEM((tm, d), BF16), pltpu.VMEM((n_a, tm, tf), BF16),
                        pltpu.VMEM((n_b, tm, tn), F32)],
        compiler_params=_params(2),
        name="channel_mlp",
    )(x, g1.reshape(1, d), mod, mod, w_up, w_down, g2.reshape(1, d), mod)


def _strict_upper(n):
    r = lax.broadcasted_iota(jnp.int32, (n, n), 0)
    c = lax.broadcasted_iota(jnp.int32, (n, n), 1)
    return jnp.where(r > c, 1.0, 0.0).astype(BF16)


def _sb_prompt_kernel(q_ref, k_ref, v_ref, o_ref, kb_ref, vb_ref, lb_ref, hi_ref, lo_ref,
                      acc_ref, carry_ref, sum_ref, *, t, dh):
    qi = pl.program_id(2)
    heads = range(q_ref.shape[1] // dh)
    cols = [slice(h * dh, (h + 1) * dh) for h in heads]

    @pl.when(qi == 0)
    def _():
        kb_ref[...] = k_ref[...].astype(BF16)
        vb_ref[...] = v_ref[...].astype(BF16)

    u = _strict_upper(t)
    qs = [q_ref[:, cs] for cs in cols]

    def logits(start):
        return [_dot_nt(q, kb_ref[pl.ds(start, t), cs]) for q, cs in zip(qs, cols)]

    def stage1(zs, slot, diagonal):
        sps = [_softplus(z) for z in zs]
        if diagonal:
            r = lax.broadcasted_iota(jnp.int32, (t, t), 0)
            c = lax.broadcasted_iota(jnp.int32, (t, t), 1)
            negs = [jnp.where(c < r, -sp, 0.0) for sp in sps]
            log_betas = [jnp.where(c < r, z - sp, MASKED_LOG) for z, sp in zip(zs, sps)]
        else:
            negs = [-sp for sp in sps]
            log_betas = [z - sp for z, sp in zip(zs, sps)]
        for h in heads:
            lb_ref[slot, h] = log_betas[h]
            hi_ref[slot, h], lo_ref[slot, h] = _split_bf16(negs[h])
            sum_ref[h] = jnp.sum(negs[h], axis=-1, keepdims=True)

    def suffixes(slot):
        return [_dot(hi_ref[slot, h], u) + _dot(lo_ref[slot, h], u) for h in heads]

    def stage2(sufs, start, slot):
        ws = [jnp.exp(lb_ref[slot, h] + sufs[h] + carry_ref[h]).astype(BF16) for h in heads]
        for h in heads:
            acc_ref[h] += _dot(ws[h], vb_ref[pl.ds(start, t), cols[h]])

    acc_ref[...] = jnp.zeros_like(acc_ref)
    carry_ref[...] = jnp.zeros_like(carry_ref)
    stage1(logits(pl.multiple_of(qi * t, t)), 0, True)

    def body(it, _):
        slot = it % 2
        sufs = suffixes(slot)
        zs = logits(pl.multiple_of((qi - 1 - it) * t, t))
        stage2(sufs, pl.multiple_of((qi - it) * t, t), slot)
        for h in heads:
            carry_ref[h] += sum_ref[h]
        stage1(zs, 1 - slot, False)
        return 0

    lax.fori_loop(0, qi, body, 0)
    stage2(suffixes(qi % 2), 0, qi % 2)
    for h in heads:
        o_ref[:, cols[h]] = acc_ref[h].astype(o_ref.dtype)


def _sb_prompt(q, k, v, layer_i, n_heads, t_pref=256, heads_per_step=4):
    b, l, w = q.shape
    dh = w // n_heads
    t = min(t_pref, l)
    hs = min(heads_per_step, n_heads)
    assert l % t == 0 and n_heads % hs == 0
    q_spec = pl.BlockSpec((None, t, hs * dh), lambda bi, h, qi: (bi, qi, h))
    kv_spec = pl.BlockSpec((None, None, l, hs * dh), lambda bi, h, qi: (layer_i, bi, 0, h))
    return pl.pallas_call(
        functools.partial(_sb_prompt_kernel, t=t, dh=dh),
        out_shape=jax.ShapeDtypeStruct((b, l, w), BF16),
        grid=(b, n_heads // hs, l // t),
        in_specs=[q_spec, kv_spec, kv_spec],
        out_specs=q_spec,
        scratch_shapes=[pltpu.VMEM((l, hs * dh), BF16), pltpu.VMEM((l, hs * dh), BF16),
                        pltpu.VMEM((2, hs, t, t), F32), pltpu.VMEM((2, hs, t, t), BF16),
                        pltpu.VMEM((2, hs, t, t), BF16), pltpu.VMEM((hs, t, dh), F32),
                        pltpu.VMEM((hs, t, 1), F32), pltpu.VMEM((hs, t, 1), F32)],
        compiler_params=_params(3),
        name="stick_breaking_prompt",
    )(q, k, v)


def _sb_sample_kernel(q_ref, kn_ref, vn_ref, kc_ref, vc_ref, o_ref, *, tk, n_heads):
    lq, w = q_ref.shape
    dh = kc_ref.shape[1]
    past = kc_ref.shape[0] // n_heads
    nl = n_heads * lq

    qf = q_ref[...].astype(F32)
    row_h = lax.broadcasted_iota(jnp.int32, (nl, w), 0) // lq
    col_h = lax.broadcasted_iota(jnp.int32, (nl, w), 1) // dh
    q_exp = jnp.where(row_h == col_h, jnp.concatenate([qf] * n_heads, axis=0), 0.0).astype(BF16)

    def strict_lower_t(n):
        r = lax.broadcasted_iota(jnp.int32, (n, n), 0)
        c = lax.broadcasted_iota(jnp.int32, (n, n), 1)
        return jnp.where(c > r, 1.0, 0.0).astype(BF16)

    def block(k_heads, v_heads, carry, accs, new):
        n = k_heads[0].shape[0]
        z = _dot_nt(k_heads[0], q_exp[:, 0:dh])
        for h in range(1, n_heads):
            z = z + _dot_nt(k_heads[h], q_exp[:, h * dh:(h + 1) * dh])
        sp = _softplus(z)
        if new:
            key = lax.broadcasted_iota(jnp.int32, (n, nl), 0)
            qry = lax.broadcasted_iota(jnp.int32, (n, nl), 1) % lq
            mask = key < qry
            neg = jnp.where(mask, -sp, 0.0)
        else:
            neg = -sp
        hi, lo = _split_bf16(neg)
        ut = strict_lower_t(n)
        suffix = _dot(ut, hi) + _dot(ut, lo)
        wgt = jnp.exp(z - sp + suffix + carry)
        if new:
            wgt = jnp.where(mask, wgt, 0.0)
        wt = wgt.T.astype(BF16)
        accs = [acc + _dot(wt[h * lq:(h + 1) * lq, :], v_heads[h]) for h, acc in enumerate(accs)]
        carry = carry + jnp.sum(neg, axis=0, keepdims=True)
        return carry, accs

    pad = jnp.zeros((LANES - lq, w), F32)
    kn = jnp.concatenate([kn_ref[...], pad], axis=0).astype(BF16)
    vn = jnp.concatenate([vn_ref[...], pad], axis=0).astype(BF16)
    heads = lambda a: [a[:, h * dh:(h + 1) * dh] for h in range(n_heads)]
    carry, accs = block(heads(kn), heads(vn), jnp.zeros((1, nl), F32),
                        [jnp.zeros((lq, dh), F32)] * n_heads, True)
    for i in reversed(range(past // tk)):
        rows = [pl.ds(i * tk * n_heads + h, tk, stride=n_heads) for h in range(n_heads)]
        ks = [kc_ref[r, :].astype(BF16) for r in rows]
        vs = [vc_ref[r, :].astype(BF16) for r in rows]
        carry, accs = block(ks, vs, carry, accs, False)

    o_ref[...] = jnp.concatenate(accs, axis=1).astype(o_ref.dtype)


def _sb_sample(q, k_new, v_new, cache_k, cache_v, layer_i, tk_pref=256):
    b, lq, w = q.shape
    _, _, past, n_heads, dh = cache_k.shape
    tk = min(tk_pref, past)
    assert past % tk == 0 and lq <= LANES
    new_spec = pl.BlockSpec((None, lq, w), lambda bi: (bi, 0, 0))
    kv_new_spec = pl.BlockSpec((None, None, lq, w), lambda bi: (layer_i, bi, 0, 0))
    cache_spec = pl.BlockSpec((None, None, past * n_heads, dh), lambda bi: (layer_i, bi, 0, 0))
    flat = lambda c: c.reshape(c.shape[0], c.shape[1], past * n_heads, dh)
    return pl.pallas_call(
        functools.partial(_sb_sample_kernel, tk=tk, n_heads=n_heads),
        out_shape=jax.ShapeDtypeStruct((b, lq, w), BF16),
        grid=(b,),
        in_specs=[new_spec, kv_new_spec, kv_new_spec, cache_spec, cache_spec],
        out_specs=new_spec,
        compiler_params=_params(1),
        name="stick_breaking_sample",
    )(q, k_new, v_new, flat(cache_k), flat(cache_v))


def _sgu_kernel(u_ref, gv_ref, g_ref, w_ref, b_ref, o_ref, *rest, r, gd):
    vb_ref = rest[-1]
    rb, width = u_ref.shape
    vn = _rms(jax.nn.gelu(gv_ref[...]), g_ref[...])
    if len(rest) == 2:
        rest[0][...] = vn
    vb_ref[...] = vn.astype(BF16)

    def chunk(s, _):
        rs = pl.ds(pl.multiple_of(s * r, r), r)
        for gi in range(width // gd):
            cs = slice(gi * gd, (gi + 1) * gd)
            mix = _dot(w_ref[gi], vb_ref[rs, cs]) + b_ref[:, cs]
            o_ref[rs, cs] = (jax.nn.gelu(u_ref[rs, cs].astype(F32)) * mix).astype(o_ref.dtype)
        return 0

    lax.fori_loop(0, rb // r, chunk, 0)


def _sgu(u, gv, g_sgu, w_mix, bias, layer_i, emit_vn, rb_pref=512):
    n, width = u.shape
    _, groups, r, _ = w_mix.shape
    rb = max(r, min(rb_pref, n))
    assert n % rb == 0 and rb % r == 0
    row_spec = pl.BlockSpec((rb, width), lambda i: (i, 0))
    out_shape = [jax.ShapeDtypeStruct((n, width), BF16)]
    out_specs = [row_spec]
    if emit_vn:
        out_shape.append(jax.ShapeDtypeStruct((n, width), F32))
        out_specs.append(row_spec)
    res = pl.pallas_call(
        functools.partial(_sgu_kernel, r=r, gd=width // groups),
        out_shape=out_shape,
        grid=(n // rb,),
        in_specs=[
            row_spec, row_spec,
            pl.BlockSpec((1, width), lambda i: (0, 0)),
            pl.BlockSpec((None, groups, r, r), lambda i: (layer_i, 0, 0, 0)),
            pl.BlockSpec((None, r, width), lambda i: (layer_i, 0, 0)),
        ],
        out_specs=out_specs,
        scratch_shapes=[pltpu.VMEM((rb, width), BF16)],
        compiler_params=_params(1),
        name="spatial_gating",
    )(u, gv, g_sgu.reshape(1, width), w_mix, bias)
    return res if emit_vn else (res[0], None)


def _gla_kernel(*refs, lc, has_state, has_carried, q_scale):
    q_ref, k_ref, v_ref, r_ref, a_ref, wgh_ref, wgl_ref, bg_ref, gg_ref = refs[:9]
    s0_ref = refs[9] if has_state else None
    o_ref, s_ref, st_ref, lg_ref = refs[9 + has_state + has_carried:]
    tb = pl.program_id(2)
    n_rows, dk = q_ref.shape
    n_chunks = n_rows // lc

    @pl.when(tb == 0)
    def _():
        if has_state:
            st_ref[...] = s0_ref[...].T
        else:
            st_ref[...] = jnp.zeros_like(st_ref)

    a_hi, a_lo = _split_bf16(a_ref[...])
    pre = (_dot(a_hi, wgh_ref[...]) + _dot(a_lo, wgh_ref[...]) + _dot(a_hi, wgl_ref[...])
           + bg_ref[...])
    lg_ref[...] = -_softplus(-pre) * (1.0 / GLA_TAU)

    row = lax.broadcasted_iota(jnp.int32, (lc, lc), 0)
    col = lax.broadcasted_iota(jnp.int32, (lc, lc), 1)
    lower = jnp.where(col <= row, 1.0, 0.0).astype(BF16)

    chunks = [slice(c * lc, (c + 1) * lc) for c in range(n_chunks)]
    per_chunk = lambda x: [x[cs] for cs in chunks]

    def roll_in_group(x, d):
        x3 = x.reshape(n_rows // SUBLANES, SUBLANES, x.shape[1])
        return pltpu.roll(x3, d, 1).reshape(x.shape)

    lg_hi, lg_lo = _split_bf16(lg_ref[...])
    b = jnp.concatenate(
        [_dot(lower, hi) + _dot(lower, lo) for hi, lo in zip(per_chunk(lg_hi), per_chunk(lg_lo))],
        axis=0)
    q = q_ref[...].astype(F32) * q_scale
    k = k_ref[...].astype(F32)

    scores = [jnp.zeros((lc, lc), F32)] * n_chunks
    h = lc // 2
    while h >= SUBLANES:
        b3 = b.reshape(n_rows // (2 * h), 2 * h, dk)
        f = jnp.exp(-jnp.abs(b3 - b3[:, h - 1:h, :])).reshape(n_rows, dk)
        qf, kf = (q * f).astype(BF16), (k * f).astype(BF16)
        mask = ((row // h) == (col // h) + 1) & ((col // h) % 2 == 0)
        scores = [sc + jnp.where(mask, _dot_nt(qc, kc), 0.0)
                  for sc, qc, kc in zip(scores, per_chunk(qf), per_chunk(kf))]
        h //= 2
    for d in range(SUBLANES):
        if d == 0:
            pd = jnp.sum(q * k, axis=-1, keepdims=True)
        else:
            kd = roll_in_group(k, d)
            bd = roll_in_group(b, d)
            pd = jnp.sum(q * kd * jnp.exp(jnp.minimum(b - bd, 0.0)), axis=-1, keepdims=True)
        mask = (col == row - d) & ((row % SUBLANES) >= d)
        scores = [sc + jnp.where(mask, pc, 0.0) for sc, pc in zip(scores, per_chunk(pd))]

    b_last = jnp.broadcast_to(b.reshape(n_chunks, lc, dk)[:, lc - 1:lc, :], (n_chunks, lc, dk))
    decay = [jnp.exp(bl[0:1, :]) for bl in per_chunk(b_last.reshape(n_rows, dk))]
    qd = per_chunk((q * jnp.exp(b)).astype(BF16))
    k_dec = per_chunk((k * jnp.exp(b_last.reshape(n_rows, dk) - b)).astype(BF16))
    vs = [v_ref[cs, :] for cs in chunks]
    intra = [_dot(sc.astype(BF16), vc) for sc, vc in zip(scores, vs)]
    kv = [_dot_tn(vc, kc) for vc, kc in zip(vs, k_dec)]

    st = st_ref[...]
    o_inter = _dot_nt(qd[0], st.astype(BF16))
    for c, cs in enumerate(chunks):
        o = o_inter + intra[c]
        st = st * decay[c] + kv[c]
        if c + 1 < n_chunks:
            o_inter = _dot_nt(qd[c + 1], st.astype(BF16))
        on = _rms(o, gg_ref[...])
        o_ref[cs, :] = (on * _silu(r_ref[cs, :].astype(F32))).astype(o_ref.dtype)
    st_ref[...] = st

    @pl.when(tb == pl.num_programs(2) - 1)
    def _():
        s_ref[...] = st_ref[...].T


def _gla(q, k, v, r, a, wg_hi, wg_lo, b_gate, g_gla, state, layer_i, n_heads, states_out,
         tb_pref=512):
    b, l, wq = q.shape
    dk = wq // n_heads
    dv = v.shape[2] // n_heads
    ra = a.shape[2]
    lc = min(GLA_CHUNK, l)
    tb = min(tb_pref, l)
    assert l % tb == 0 and tb % lc == 0 and lc % (2 * SUBLANES) == 0
    qk_spec = pl.BlockSpec((None, tb, dk), lambda bi, h, t: (bi, t, h))
    vr_spec = pl.BlockSpec((None, tb, dv), lambda bi, h, t: (bi, t, h))
    in_specs = [
        qk_spec, qk_spec, vr_spec, vr_spec,
        pl.BlockSpec((None, tb, ra), lambda bi, h, t: (bi, t, 0)),
        pl.BlockSpec((None, ra, dk), lambda bi, h, t: (layer_i, 0, h)),
        pl.BlockSpec((None, ra, dk), lambda bi, h, t: (layer_i, 0, h)),
        pl.BlockSpec((None, 1, dk), lambda bi, h, t: (layer_i, 0, h)),
        pl.BlockSpec((None, 1, dv), lambda bi, h, t: (layer_i, 0, 0)),
    ]
    args = [q, k, v, r, a, wg_hi, wg_lo, b_gate.reshape(-1, 1, wq), g_gla.reshape(-1, 1, dv)]
    has_state = state is not None
    if has_state:
        in_specs.append(
            pl.BlockSpec((None, None, None, dk, dv), lambda bi, h, t: (layer_i, bi, h, 0, 0)))
        args.append(state)
    aliases = {}
    if states_out is not None:
        aliases[len(args)] = 1
        in_specs.append(pl.BlockSpec(memory_space=pl.ANY))
        args.append(states_out)
    n_layers = b_gate.shape[0]
    return pl.pallas_call(
        functools.partial(_gla_kernel, lc=lc, has_state=has_state,
                          has_carried=states_out is not None, q_scale=dk ** -0.5),
        out_shape=[jax.ShapeDtypeStruct((b, l, n_heads * dv), BF16),
                   jax.ShapeDtypeStruct((n_layers, b, n_heads, dk, dv), F32)],
        grid=(b, n_heads, l // tb),
        in_specs=in_specs,
        out_specs=[vr_spec, pl.BlockSpec((None, None, None, dk, dv),
                                         lambda bi, h, t: (layer_i, bi, h, 0, 0))],
        scratch_shapes=[pltpu.VMEM((dv, dk), F32), pltpu.VMEM((tb, dk), F32)],
        input_output_aliases=aliases,
        compiler_params=_params(3),
        name="gated_linear_attention",
    )(*args)


def _trunk(x3, grp, mod, weights, dims, cache_k, cache_v, state_gla):
    (g_norm, w_in_even, w_out_even, g_sgu, sgu_mix, sgu_bias, w_in_odd, w_a, wg, b_gate, g_gla,
     w_out_odd, w_ff_up, w_ff_down) = weights
    sb_heads, sb_width, sgu_width, gla_heads, gla_qk, gla_vw = dims
    n_seq, seq_len, d = x3.shape
    sample = cache_k is not None
    x = x3.reshape(n_seq * seq_len, d)
    seq = lambda t: t.reshape(n_seq, seq_len, t.shape[-1])
    depth = g_norm.shape[0]
    n_even = (depth + 1) // 2
    k_all = jnp.zeros((n_even, x.shape[0], sb_width), F32)
    v_all = jnp.zeros((n_even, x.shape[0], sb_width), F32)
    states = jnp.zeros((depth // 2, n_seq, gla_heads, gla_qk // gla_heads, gla_vw // gla_heads), F32)
    gvs = []
    for layer in range(depth):
        i = layer // 2
        if layer % 2 == 0:
            sb_scale = (sb_width // sb_heads) ** -0.5
            q, k_all, v_all, u, gv = _inproj(
                x, mod, layer, g_norm[layer, 0], w_in_even, i,
                [(sb_width, BF16, sb_scale, None), (sb_width, F32, 1.0, (n_even, k_all)),
                 (sb_width, F32, 1.0, (n_even, v_all)), (sgu_width, BF16, 1.0, None),
                 (sgu_width, F32, 1.0, None)], grp)
            kv_seq = lambda t: t.reshape(n_even, n_seq, seq_len, sb_width)
            if sample:
                o_a = _sb_sample(seq(q), kv_seq(k_all), kv_seq(v_all), cache_k, cache_v, i)
            else:
                o_a = _sb_prompt(seq(q), kv_seq(k_all), kv_seq(v_all), i, sb_heads)
            o_b, vn = _sgu(u, gv, g_sgu[i], sgu_mix, sgu_bias, i, emit_vn=sample)
            x = _outproj([o_a.reshape(x.shape[0], sb_width), o_b], w_out_even, i, x, mod, layer,
                         g_norm[layer, 1], grp)
            gvs.append(vn)
        else:
            q, k, v, r, a = _inproj(
                x, mod, layer, g_norm[layer, 0], w_in_odd, i,
                [(gla_qk, BF16, 1.0, None), (gla_qk, BF16, 1.0, None), (gla_vw, BF16, 1.0, None),
                 (gla_vw, BF16, 1.0, None)], grp, aux_w=w_a)
            o, states = _gla(seq(q), seq(k), seq(v), seq(r), seq(a), wg[0], wg[1], b_gate, g_gla,
                             state_gla if sample else None, i, gla_heads, states)
            x = _outproj([o.reshape(x.shape[0], gla_vw)], w_out_odd, i, x, mod, layer,
                         g_norm[layer, 1], grp)
        x = _ff(x, mod, layer, g_norm[layer, 2], g_norm[layer, 3], w_ff_up, w_ff_down, grp)
    return x.reshape(n_seq, seq_len, d), k_all, v_all, states, gvs


def kernel(x_prompt, x_sample, cache_sb_k, cache_sb_v, state_gla, c_prompt, c_sample, w_mod, b_mod, g_norm, w_in_even, w_out_even, g_sgu, w_sgu, b_sgu, w_in_odd, w_gate_up, b_gate, g_gla, w_out_odd, w_ff_up, w_ff_down):
    batch, seq, d = x_prompt.shape
    dec_batch, dec_seq, _ = x_sample.shape
    n_even, _, past, sb_heads, sb_dh = cache_sb_k.shape
    sb_width = sb_heads * sb_dh
    _, _, gla_heads, gla_dk, gla_dv = state_gla.shape
    gla_qk, gla_vw = gla_heads * gla_dk, gla_heads * gla_dv
    _, groups, sgu_chunk, _ = w_sgu.shape
    sgu_width = g_sgu.shape[1]
    rank = w_gate_up.shape[1]
    depth, d_ff = w_ff_up.shape[0], w_ff_up.shape[2]
    dims = (sb_heads, sb_width, sgu_width, gla_heads, gla_qk, gla_vw)

    mod = _modulation(jnp.concatenate([c_sample, c_prompt], axis=0), w_mod, b_mod)
    mod = mod.reshape(mod.shape[0], mod.shape[1], 1, mod.shape[2])

    cast = lambda t: t.astype(BF16)
    n_main = 2 * gla_qk + 2 * gla_vw
    w_a = jnp.pad(w_in_odd[:, :, n_main:], ((0, 0), (0, 0), (0, LANES - rank)))
    w_a_hi = cast(w_a)
    w_a = (w_a_hi, cast(w_a - w_a_hi.astype(F32)))
    wg = jnp.pad(w_gate_up, ((0, 0), (0, LANES - rank), (0, 0)))
    wg_hi = cast(wg)
    wg = (wg_hi, cast(wg - wg_hi.astype(F32)))
    w_tri = w_sgu * jnp.tril(jnp.ones((sgu_chunk, sgu_chunk), F32))

    def sgu_tables(chunk_len):
        mix = cast(w_tri[:, :, :chunk_len, :chunk_len])
        bias = jnp.repeat(jnp.swapaxes(b_sgu[:, :, :chunk_len], 1, 2), sgu_width // groups, axis=2)
        return mix, bias

    common = (g_norm, cast(w_in_even), cast(w_out_even), g_sgu)
    tail = (cast(w_in_odd), w_a, wg, b_gate, g_gla, cast(w_out_odd), cast(w_ff_up), cast(w_ff_down))

    grp_p = _Group(batch, seq, dec_batch, 512)
    grp_s = _Group(dec_batch, dec_seq, 0, 512)

    y_p, ks_p, vs_p, st_p, _ = _trunk(
        x_prompt, grp_p, mod, common + sgu_tables(sgu_chunk) + tail, dims, None, None, None)
    y_s, ks_s, vs_s, st_s, gv_s = _trunk(
        x_sample, grp_s, mod, common + sgu_tables(dec_seq) + tail, dims,
        cache_sb_k, cache_sb_v, state_gla)

    heads = lambda t, b, l: t.reshape(t.shape[0], b, l, sb_heads, sb_dh)
    return (y_p, y_s,
            heads(ks_p, batch, seq), heads(vs_p, batch, seq), st_p,
            heads(ks_s, dec_batch, dec_seq), heads(vs_s, dec_batch, dec_seq), st_s,
            jnp.stack(gv_s).reshape(len(gv_s), dec_batch, dec_seq, sgu_width))
```

```python
import functools

import jax
import jax.numpy as jnp
from jax import lax
from jax.experimental import pallas as pl
from jax.experimental.pallas import tpu as pltpu

F32 = jnp.float32
BF16 = jnp.bfloat16

EPS = 1e-6
GLA_CHUNK = 64
GLA_TAU = 16.0
SUBLANES = 8
GLA_DIRECT = 2
LANES = 128
MASKED_LOG = -1e30
VMEM_LIMIT_BYTES = 56 * 1024 * 1024


def _params(n_grid):
    return pltpu.CompilerParams(
        dimension_semantics=("arbitrary",) * n_grid,
        vmem_limit_bytes=VMEM_LIMIT_BYTES)


def _dot(a, b):
    return jnp.dot(a, b, preferred_element_type=F32)


def _dot_nt(a, b):
    return lax.dot_general(a, b, (((1,), (1,)), ((), ())), preferred_element_type=F32)


def _dot_tn(a, b):
    return lax.dot_general(a, b, (((0,), (0,)), ((), ())), preferred_element_type=F32)


def _split_bf16(x):
    hi = x.astype(BF16)
    lo = (x - hi.astype(F32)).astype(BF16)
    return hi, lo


def _rms(x, g):
    return x * lax.rsqrt(jnp.mean(x * x, axis=-1, keepdims=True) + EPS) * g


def _softplus(z):
    return jnp.maximum(z, 0.0) + jnp.log(1.0 + jnp.exp(-jnp.abs(z)))


def _silu(x):
    return x * jax.nn.sigmoid(x)


def _rows(ref, rows):
    v = ref[...]
    nb, _, d = v.shape
    if nb == 1:
        return v[0]
    return jnp.broadcast_to(v, (nb, rows, d)).reshape(nb * rows, d)


class _Group:
    def __init__(self, n_seq, seq_len, mod_row0, tm_pref):
        self.n_seq, self.seq_len, self.mod_row0 = n_seq, seq_len, mod_row0
        self.n = n_seq * seq_len
        if seq_len >= tm_pref:
            assert seq_len % tm_pref == 0
            self.tm, self.nb = tm_pref, 1
        else:
            self.nb = min(n_seq, tm_pref // seq_len)
            assert n_seq % self.nb == 0
            self.tm = self.nb * seq_len
        self.rows = self.tm // self.nb
        self.blocks_per_seq = max(1, seq_len // self.tm)

    def mod_spec(self, layer, col, d):
        nb, bps, row0 = self.nb, self.blocks_per_seq, self.mod_row0

        def index(i, *_):
            if nb == 1:
                return (layer, row0 + i // bps, 0, col)
            return (layer, row0 // nb + i, 0, col)

        return pl.BlockSpec((None, nb, 1, d), index)


def _mod_kernel(c_ref, w_ref, b_ref, o_ref):
    cs = _silu(c_ref[...]).astype(BF16)
    o_ref[...] = _dot(cs, w_ref[...].astype(BF16)) + b_ref[...]


def _modulation(c_all, w_mod, b_mod):
    depth, d, n_out = w_mod.shape
    n_c = c_all.shape[0]
    tn = min(n_out, 1024)
    return pl.pallas_call(
        _mod_kernel,
        out_shape=jax.ShapeDtypeStruct((depth, n_c, n_out), F32),
        grid=(depth, n_out // tn),
        in_specs=[
            pl.BlockSpec((n_c, d), lambda l, j: (0, 0)),
            pl.BlockSpec((None, d, tn), lambda l, j: (l, 0, j)),
            pl.BlockSpec((None, 1, tn), lambda l, j: (l, 0, j)),
        ],
        out_specs=pl.BlockSpec((None, n_c, tn), lambda l, j: (l, 0, j)),
        compiler_params=_params(2),
        name="modulation",
    )(c_all, w_mod, b_mod.reshape(depth, 1, n_out))


def _inproj_kernel(*refs, seg_cols, seg_scales, rows, has_aux, n_carried, n_slabs, layer_i):
    x_ref, g_ref, sh_ref, sc_ref, w_ref = refs[:5]
    pos = 5
    if has_aux:
        wah_ref, wal_ref = refs[5:7]
        pos = 7
    pos += n_carried
    out_refs = refs[pos:pos + len(seg_cols)]
    if has_aux:
        aux_ref = refs[pos + len(seg_cols)]
    tm = x_ref.shape[0]
    scale_rows, shift_rows = _rows(sc_ref, rows), _rows(sh_ref, rows)
    for s in range(n_slabs):
        rs = slice(s * tm // n_slabs, (s + 1) * tm // n_slabs)
        per_row = lambda v: v if v.shape[0] == 1 else v[rs]
        h = _rms(x_ref[rs, :], g_ref[...]) * (1.0 + per_row(scale_rows)) + per_row(shift_rows)
        h_hi, h_lo = _split_bf16(h)
        if has_aux:
            aux_ref[rs, :] = (_dot(h_hi, wah_ref[...]) + _dot(h_lo, wah_ref[...])
                              + _dot(h_hi, wal_ref[...]))
        for o_ref, (c0, c1), scale in zip(out_refs, seg_cols, seg_scales):
            y = _dot(h_hi, w_ref[:, c0:c1])
            if scale != 1.0:
                y = y * scale
            if len(o_ref.shape) == 2:
                o_ref[rs, :] = y.astype(o_ref.dtype)
            else:
                for l in range(o_ref.shape[0]):
                    o_ref[l, rs, :] = (y if l == layer_i else jnp.zeros_like(y)).astype(o_ref.dtype)


def _inproj(x, mod, layer, g, w, layer_i, segs, grp, aux_w=None):
    n, d = x.shape
    tm = grp.tm
    w_cols = w.shape[2]
    in_specs = [
        pl.BlockSpec((tm, d), lambda i: (i, 0)),
        pl.BlockSpec((1, d), lambda i: (0, 0)),
        grp.mod_spec(layer, 0, d),
        grp.mod_spec(layer, 1, d),
        pl.BlockSpec((None, d, w_cols), lambda i: (layer_i, 0, 0), pipeline_mode=pl.Buffered(1)),
    ]
    args = [x, g.reshape(1, d), mod, mod, w]
    has_aux = aux_w is not None
    if has_aux:
        wah, wal = aux_w
        wa = wah.shape[2]
        in_specs += [pl.BlockSpec((None, d, wa), lambda i: (layer_i, 0, 0))] * 2
        args += [wah, wal]
    out_shape, out_specs, aliases, seg_cols = [], [], {}, []
    c0 = 0
    for k, (wd, dt, _, stack) in enumerate(segs):
        seg_cols.append((c0, c0 + wd))
        c0 += wd
        if stack is None:
            out_shape.append(jax.ShapeDtypeStruct((n, wd), dt))
            out_specs.append(pl.BlockSpec((tm, wd), lambda i: (i, 0)))
        else:
            n_layers, previous = stack
            out_shape.append(jax.ShapeDtypeStruct((n_layers, n, wd), dt))
            if previous is None:
                out_specs.append(pl.BlockSpec((n_layers, tm, wd), lambda i: (0, i, 0)))
            else:
                out_specs.append(pl.BlockSpec((None, tm, wd), lambda i: (layer_i, i, 0)))
                aliases[len(args)] = k
                in_specs.append(pl.BlockSpec(memory_space=pl.ANY))
                args.append(previous)
    assert c0 <= w_cols and w.shape[1] == d
    if has_aux:
        out_shape.append(jax.ShapeDtypeStruct((n, wa), F32))
        out_specs.append(pl.BlockSpec((tm, wa), lambda i: (i, 0)))
    return pl.pallas_call(
        functools.partial(_inproj_kernel, seg_cols=tuple(seg_cols),
                          seg_scales=tuple(sg[2] for sg in segs), rows=grp.rows, has_aux=has_aux,
                          n_carried=len(aliases), n_slabs=2 if tm % 32 == 0 else 1,
                          layer_i=layer_i),
        out_shape=out_shape,
        grid=(n // tm,),
        in_specs=in_specs,
        out_specs=out_specs,
        input_output_aliases=aliases,
        compiler_params=_params(1),
        name="in_proj",
    )(*args)


def _outproj_kernel(*refs, n_a, rows):
    a_refs = refs[:n_a]
    w_refs = refs[n_a:2 * n_a]
    x_ref, g_ref, gt_ref, o_ref = refs[2 * n_a:]
    tm = x_ref.shape[0]
    n_slabs = 2 if tm % 32 == 0 else 1
    gate_rows = _rows(gt_ref, rows)
    for s in range(n_slabs):
        rs = slice(s * tm // n_slabs, (s + 1) * tm // n_slabs)
        y = _dot(a_refs[0][rs, :], w_refs[0][...])
        for a_ref, w_ref in zip(a_refs[1:], w_refs[1:]):
            y = y + _dot(a_ref[rs, :], w_ref[...])
        gate = gate_rows if gate_rows.shape[0] == 1 else gate_rows[rs]
        o_ref[rs, :] = x_ref[rs, :] + gate * _rms(y, g_ref[...])


def _outproj(acts, w, layer_i, x, mod, layer, g, grp):
    n, d = x.shape
    tm = grp.tm
    ka = acts[0].shape[1]
    assert all(a.shape[1] == ka for a in acts) and w.shape[1] == ka * len(acts)
    in_specs = [pl.BlockSpec((tm, ka), lambda i: (i, 0)) for _ in acts]
    in_specs += [pl.BlockSpec((None, ka, d), lambda i, r=r: (layer_i, r, 0)) for r in range(len(acts))]
    in_specs += [
        pl.BlockSpec((tm, d), lambda i: (i, 0)),
        pl.BlockSpec((1, d), lambda i: (0, 0)),
        grp.mod_spec(layer, 2, d),
    ]
    return pl.pallas_call(
        functools.partial(_outproj_kernel, n_a=len(acts), rows=grp.rows),
        out_shape=jax.ShapeDtypeStruct((n, d), F32),
        grid=(n // tm,),
        in_specs=in_specs,
        out_specs=pl.BlockSpec((tm, d), lambda i: (i, 0)),
        compiler_params=_params(1),
        name="out_proj",
    )(*acts, *([w] * len(acts)), x, g.reshape(1, d), mod)


def _ff_kernel(x_ref, g1_ref, sh_ref, sc_ref, wu_ref, wd_ref, g2_ref, gt_ref, o_ref,
               h_ref, t_ref, y_ref, *, rows, n_a, n_b):
    j = pl.program_id(1)
    tf = t_ref.shape[2]
    tn = y_ref.shape[2]

    def hidden(h):
        t = jnp.maximum(_dot(h, wu_ref[...]), 0.0)
        return (t * t).astype(BF16)

    @pl.when(j == 0)
    def _():
        tm = x_ref.shape[0]
        n_slabs = 2 if tm % 32 == 0 else 1
        scale_rows, shift_rows = _rows(sc_ref, rows), _rows(sh_ref, rows)
        for s in range(n_slabs):
            rs = slice(s * tm // n_slabs, (s + 1) * tm // n_slabs)
            per_row = lambda v: v if v.shape[0] == 1 else v[rs]
            h = _rms(x_ref[rs, :], g1_ref[...]) * (1.0 + per_row(scale_rows)) + per_row(shift_rows)
            h = h.astype(BF16)
            h_ref[rs, :] = h
            t_ref[0, rs, :] = hidden(h)

    @pl.when((j > 0) & (j < n_a))
    def _():
        t_ref[j] = hidden(h_ref[...])

    @pl.when(j >= n_a)
    def _():
        acc = _dot(t_ref[0], wd_ref[0:tf, :])
        for a in range(1, n_a):
            acc = acc + _dot(t_ref[a], wd_ref[a * tf:(a + 1) * tf, :])
        y_ref[j - n_a] = acc

    @pl.when(j == n_a + n_b - 1)
    def _():
        ss = jnp.sum(jnp.square(y_ref[0]), axis=-1, keepdims=True)
        for b in range(1, n_b):
            ss = ss + jnp.sum(jnp.square(y_ref[b]), axis=-1, keepdims=True)
        rs = lax.rsqrt(ss * (1.0 / (n_b * tn)) + EPS)
        gate = _rows(gt_ref, rows)
        g2 = g2_ref[...]
        for b in range(n_b):
            cs = slice(b * tn, (b + 1) * tn)
            o_ref[:, cs] = x_ref[:, cs] + gate[:, cs] * (y_ref[b] * rs * g2[:, cs])


def _ff(x, mod, layer, g1, g2, w_up, w_down, grp):
    n, d = x.shape
    f = w_up.shape[2]
    tm = grp.tm
    tf = min(1024, f)
    tn = min(2 * LANES, d)
    assert f % tf == 0 and d % tn == 0
    n_a, n_b = f // tf, d // tn
    return pl.pallas_call(
        functools.partial(_ff_kernel, rows=grp.rows, n_a=n_a, n_b=n_b),
        out_shape=jax.ShapeDtypeStruct((n, d), F32),
        grid=(n // tm, n_a + n_b),
        in_specs=[
            pl.BlockSpec((tm, d), lambda i, j: (i, 0)),
            pl.BlockSpec((1, d), lambda i, j: (0, 0)),
            grp.mod_spec(layer, 3, d),
            grp.mod_spec(layer, 4, d),
            pl.BlockSpec((None, d, tf), lambda i, j: (layer, 0, jnp.minimum(j, n_a - 1))),
            pl.BlockSpec((None, f, tn), lambda i, j: (layer, 0, jnp.clip(j - n_a, 0, n_b - 1))),
            pl.BlockSpec((1, d), lambda i, j: (0, 0)),
            grp.mod_spec(layer, 5, d),
        ],
        out_specs=pl.BlockSpec((tm, d), lambda i, j: (i, 0)),
        scratch_shapes=[pltpu.VMEM((tm, d), BF16), pltpu.VMEM((n_a, tm, tf), BF16),
                        pltpu.VMEM((n_b, tm, tn), F32)],
        compiler_params=_params(2),
        name="channel_mlp",
    )(x, g1.reshape(1, d), mod, mod, w_up, w_down, g2.reshape(1, d), mod)


def _strict_upper(n):
    r = lax.broadcasted_iota(jnp.int32, (n, n), 0)
    c = lax.broadcasted_iota(jnp.int32, (n, n), 1)
    return jnp.where(r > c, 1.0, 0.0).astype(BF16)


def _sb_prompt_kernel(q_ref, k_ref, v_ref, o_ref, kb_ref, vb_ref, lb_ref, hi_ref, lo_ref,
                      acc_ref, carry_ref, sum_ref, *, t, dh):
    qi = pl.program_id(2)
    heads = range(q_ref.shape[1] // dh)
    cols = [slice(h * dh, (h + 1) * dh) for h in heads]

    @pl.when(qi == 0)
    def _():
        kb_ref[...] = k_ref[...].astype(BF16)
        vb_ref[...] = v_ref[...].astype(BF16)

    u = _strict_upper(t)
    qs = [q_ref[:, cs] for cs in cols]

    def logits(start):
        return [_dot_nt(q, kb_ref[pl.ds(start, t), cs]) for q, cs in zip(qs, cols)]

    def stage1(zs, slot, diagonal):
        sps = [_softplus(z) for z in zs]
        if diagonal:
            r = lax.broadcasted_iota(jnp.int32, (t, t), 0)
            c = lax.broadcasted_iota(jnp.int32, (t, t), 1)
            negs = [jnp.where(c < r, -sp, 0.0) for sp in sps]
            log_betas = [jnp.where(c < r, z - sp, MASKED_LOG) for z, sp in zip(zs, sps)]
        else:
            negs = [-sp for sp in sps]
            log_betas = [z - sp for z, sp in zip(zs, sps)]
        for h in heads:
            lb_ref[slot, h] = log_betas[h]
            hi_ref[slot, h], lo_ref[slot, h] = _split_bf16(negs[h])
            sum_ref[h] = jnp.sum(negs[h], axis=-1, keepdims=True)

    def suffixes(slot):
        return [_dot(hi_ref[slot, h], u) + _dot(lo_ref[slot, h], u) for h in heads]

    def stage2(sufs, start, slot):
        ws = [jnp.exp(lb_ref[slot, h] + sufs[h] + carry_ref[h]).astype(BF16) for h in heads]
        for h in heads:
            acc_ref[h] += _dot(ws[h], vb_ref[pl.ds(start, t), cols[h]])

    acc_ref[...] = jnp.zeros_like(acc_ref)
    carry_ref[...] = jnp.zeros_like(carry_ref)
    stage1(logits(pl.multiple_of(qi * t, t)), 0, True)

    def body(it, _):
        slot = it % 2
        sufs = suffixes(slot)
        zs = logits(pl.multiple_of((qi - 1 - it) * t, t))
        stage2(sufs, pl.multiple_of((qi - it) * t, t), slot)
        for h in heads:
            carry_ref[h] += sum_ref[h]
        stage1(zs, 1 - slot, False)
        return 0

    lax.fori_loop(0, qi, body, 0)
    stage2(suffixes(qi % 2), 0, qi % 2)
    for h in heads:
        o_ref[:, cols[h]] = acc_ref[h].astype(o_ref.dtype)


def _sb_prompt(q, k, v, layer_i, n_heads, t_pref=256, heads_per_step=4):
    b, l, w = q.shape
    dh = w // n_heads
    t = min(t_pref, l)
    hs = min(heads_per_step, n_heads)
    assert l % t == 0 and n_heads % hs == 0
    q_spec = pl.BlockSpec((None, t, hs * dh), lambda bi, h, qi: (bi, qi, h))
    kv_spec = pl.BlockSpec((None, None, l, hs * dh), lambda bi, h, qi: (layer_i, bi, 0, h))
    return pl.pallas_call(
        functools.partial(_sb_prompt_kernel, t=t, dh=dh),
        out_shape=jax.ShapeDtypeStruct((b, l, w), BF16),
        grid=(b, n_heads // hs, l // t),
        in_specs=[q_spec, kv_spec, kv_spec],
        out_specs=q_spec,
        scratch_shapes=[pltpu.VMEM((l, hs * dh), BF16), pltpu.VMEM((l, hs * dh), BF16),
                        pltpu.VMEM((2, hs, t, t), F32), pltpu.VMEM((2, hs, t, t), BF16),
                        pltpu.VMEM((2, hs, t, t), BF16), pltpu.VMEM((hs, t, dh), F32),
                        pltpu.VMEM((hs, t, 1), F32), pltpu.VMEM((hs, t, 1), F32)],
        compiler_params=_params(3),
        name="stick_breaking_prompt",
    )(q, k, v)


def _sb_sample_kernel(q_ref, kn_ref, vn_ref, kc_ref, vc_ref, o_ref, *, tk, n_heads):
    lq, w = q_ref.shape
    dh = kc_ref.shape[1]
    past = kc_ref.shape[0] // n_heads
    nl = n_heads * lq

    qf = q_ref[...].astype(F32)
    row_h = lax.broadcasted_iota(jnp.int32, (nl, w), 0) // lq
    col_h = lax.broadcasted_iota(jnp.int32, (nl, w), 1) // dh
    q_exp = jnp.where(row_h == col_h, jnp.concatenate([qf] * n_heads, axis=0), 0.0).astype(BF16)

    def strict_lower_t(n):
        r = lax.broadcasted_iota(jnp.int32, (n, n), 0)
        c = lax.broadcasted_iota(jnp.int32, (n, n), 1)
        return jnp.where(c > r, 1.0, 0.0).astype(BF16)

    def block(k_heads, v_heads, carry, accs, new):
        n = k_heads[0].shape[0]
        z = _dot_nt(k_heads[0], q_exp[:, 0:dh])
        for h in range(1, n_heads):
            z = z + _dot_nt(k_heads[h], q_exp[:, h * dh:(h + 1) * dh])
        sp = _softplus(z)
        if new:
            key = lax.broadcasted_iota(jnp.int32, (n, nl), 0)
            qry = lax.broadcasted_iota(jnp.int32, (n, nl), 1) % lq
            mask = key < qry
            neg = jnp.where(mask, -sp, 0.0)
        else:
            neg = -sp
        hi, lo = _split_bf16(neg)
        ut = strict_lower_t(n)
        suffix = _dot(ut, hi) + _dot(ut, lo)
        wgt = jnp.exp(z - sp + suffix + carry)
        if new:
            wgt = jnp.where(mask, wgt, 0.0)
        wt = wgt.T.astype(BF16)
        accs = [acc + _dot(wt[h * lq:(h + 1) * lq, :], v_heads[h]) for h, acc in enumerate(accs)]
        carry = carry + jnp.sum(neg, axis=0, keepdims=True)
        return carry, accs

    pad = jnp.zeros((LANES - lq, w), F32)
    kn = jnp.concatenate([kn_ref[...], pad], axis=0).astype(BF16)
    vn = jnp.concatenate([vn_ref[...], pad], axis=0).astype(BF16)
    heads = lambda a: [a[:, h * dh:(h + 1) * dh] for h in range(n_heads)]
    carry, accs = block(heads(kn), heads(vn), jnp.zeros((1, nl), F32),
                        [jnp.zeros((lq, dh), F32)] * n_heads, True)
    for i in reversed(range(past // tk)):
        rows = [pl.ds(i * tk * n_heads + h, tk, stride=n_heads) for h in range(n_heads)]
        ks = [kc_ref[r, :].astype(BF16) for r in rows]
        vs = [vc_ref[r, :].astype(BF16) for r in rows]
        carry, accs = block(ks, vs, carry, accs, False)

    o_ref[...] = jnp.concatenate(accs, axis=1).astype(o_ref.dtype)


def _sb_sample(q, k_new, v_new, cache_k, cache_v, layer_i, tk_pref=256):
    b, lq, w = q.shape
    _, _, past, n_heads, dh = cache_k.shape
    tk = min(tk_pref, past)
    assert past % tk == 0 and lq <= LANES
    new_spec = pl.BlockSpec((None, lq, w), lambda bi: (bi, 0, 0))
    kv_new_spec = pl.BlockSpec((None, None, lq, w), lambda bi: (layer_i, bi, 0, 0))
    cache_spec = pl.BlockSpec((None, None, past * n_heads, dh), lambda bi: (layer_i, bi, 0, 0))
    flat = lambda c: c.reshape(c.shape[0], c.shape[1], past * n_heads, dh)
    return pl.pallas_call(
        functools.partial(_sb_sample_kernel, tk=tk, n_heads=n_heads),
        out_shape=jax.ShapeDtypeStruct((b, lq, w), BF16),
        grid=(b,),
        in_specs=[new_spec, kv_new_spec, kv_new_spec, cache_spec, cache_spec],
        out_specs=new_spec,
        compiler_params=_params(1),
        name="stick_breaking_sample",
    )(q, k_new, v_new, flat(cache_k), flat(cache_v))


def _sgu_kernel(u_ref, gv_ref, g_ref, w_ref, b_ref, o_ref, *rest, r, gd):
    vb_ref = rest[-1]
    rb, width = u_ref.shape
    vn = _rms(jax.nn.gelu(gv_ref[...]), g_ref[...])
    if len(rest) == 2:
        rest[0][...] = vn
    vb_ref[...] = vn.astype(BF16)

    def chunk(s, _):
        rs = pl.ds(pl.multiple_of(s * r, r), r)
        for gi in range(width // gd):
            cs = slice(gi * gd, (gi + 1) * gd)
            mix = _dot(w_ref[gi], vb_ref[rs, cs]) + b_ref[:, cs]
            o_ref[rs, cs] = (jax.nn.gelu(u_ref[rs, cs].astype(F32)) * mix).astype(o_ref.dtype)
        return 0

    lax.fori_loop(0, rb // r, chunk, 0)


def _sgu(u, gv, g_sgu, w_mix, bias, layer_i, emit_vn, rb_pref=512):
    n, width = u.shape
    _, groups, r, _ = w_mix.shape
    rb = max(r, min(rb_pref, n))
    assert n % rb == 0 and rb % r == 0
    row_spec = pl.BlockSpec((rb, width), lambda i: (i, 0))
    out_shape = [jax.ShapeDtypeStruct((n, width), BF16)]
    out_specs = [row_spec]
    if emit_vn:
        out_shape.append(jax.ShapeDtypeStruct((n, width), F32))
        out_specs.append(row_spec)
    res = pl.pallas_call(
        functools.partial(_sgu_kernel, r=r, gd=width // groups),
        out_shape=out_shape,
        grid=(n // rb,),
        in_specs=[
            row_spec, row_spec,
            pl.BlockSpec((1, width), lambda i: (0, 0)),
            pl.BlockSpec((None, groups, r, r), lambda i: (layer_i, 0, 0, 0)),
            pl.BlockSpec((None, r, width), lambda i: (layer_i, 0, 0)),
        ],
        out_specs=out_specs,
        scratch_shapes=[pltpu.VMEM((rb, width), BF16)],
        compiler_params=_params(1),
        name="spatial_gating",
    )(u, gv, g_sgu.reshape(1, width), w_mix, bias)
    return res if emit_vn else (res[0], None)


def _gla_kernel(*refs, lc, has_state, has_carried, q_scale):
    q_ref, k_ref, v_ref, r_ref, a_ref, wgh_ref, wgl_ref, bg_ref, gg_ref = refs[:9]
    s0_ref = refs[9] if has_state else None
    o_ref, s_ref, st_ref, lg_ref = refs[9 + has_state + has_carried:]
    tb = pl.program_id(2)
    n_rows, dk = q_ref.shape
    n_chunks = n_rows // lc

    @pl.when(tb == 0)
    def _():
        if has_state:
            st_ref[...] = s0_ref[...].T
        else:
            st_ref[...] = jnp.zeros_like(st_ref)

    a_hi, a_lo = _split_bf16(a_ref[...])
    pre = (_dot(a_hi, wgh_ref[...]) + _dot(a_lo, wgh_ref[...]) + _dot(a_hi, wgl_ref[...])
           + bg_ref[...])
    lg_ref[...] = -_softplus(-pre) * (1.0 / GLA_TAU)

    row = lax.broadcasted_iota(jnp.int32, (lc, lc), 0)
    col = lax.broadcasted_iota(jnp.int32, (lc, lc), 1)
    lower = jnp.where(col <= row, 1.0, 0.0).astype(BF16)

    chunks = [slice(c * lc, (c + 1) * lc) for c in range(n_chunks)]
    per_chunk = lambda x: [x[cs] for cs in chunks]

    def roll_in_group(x, d):
        x3 = x.reshape(n_rows // SUBLANES, SUBLANES, x.shape[1])
        return pltpu.roll(x3, d, 1).reshape(x.shape)

    lg_hi, lg_lo = _split_bf16(lg_ref[...])
    b = jnp.concatenate(
        [_dot(lower, hi) + _dot(lower, lo) for hi, lo in zip(per_chunk(lg_hi), per_chunk(lg_lo))],
        axis=0)
    q = q_ref[...].astype(F32) * q_scale
    k = k_ref[...].astype(F32)

    scores = [jnp.zeros((lc, lc), F32)] * n_chunks
    h = lc // 2
    while h >= GLA_DIRECT:
        b3 = b.reshape(n_rows // (2 * h), 2 * h, dk)
        f = jnp.exp(-jnp.abs(b3 - b3[:, h - 1:h, :])).reshape(n_rows, dk)
        qf, kf = (q * f).astype(BF16), (k * f).astype(BF16)
        mask = ((row // h) == (col // h) + 1) & ((col // h) % 2 == 0)
        scores = [sc + jnp.where(mask, _dot_nt(qc, kc), 0.0)
                  for sc, qc, kc in zip(scores, per_chunk(qf), per_chunk(kf))]
        h //= 2
    for d in range(GLA_DIRECT):
        if d == 0:
            pd = jnp.sum(q * k, axis=-1, keepdims=True)
        else:
            kd = roll_in_group(k, d)
            bd = roll_in_group(b, d)
            pd = jnp.sum(q * kd * jnp.exp(jnp.minimum(b - bd, 0.0)), axis=-1, keepdims=True)
        mask = (col == row - d) & ((row % GLA_DIRECT) >= d)
        scores = [sc + jnp.where(mask, pc, 0.0) for sc, pc in zip(scores, per_chunk(pd))]

    b_last = jnp.broadcast_to(b.reshape(n_chunks, lc, dk)[:, lc - 1:lc, :], (n_chunks, lc, dk))
    decay = [jnp.exp(bl[0:1, :]) for bl in per_chunk(b_last.reshape(n_rows, dk))]
    qd = per_chunk((q * jnp.exp(b)).astype(BF16))
    k_dec = per_chunk((k * jnp.exp(b_last.reshape(n_rows, dk) - b)).astype(BF16))
    vs = [v_ref[cs, :] for cs in chunks]
    intra = [_dot(sc.astype(BF16), vc) for sc, vc in zip(scores, vs)]
    kv = [_dot_tn(vc, kc) for vc, kc in zip(vs, k_dec)]

    st = st_ref[...]
    o_inter = _dot_nt(qd[0], st.astype(BF16))
    for c, cs in enumerate(chunks):
        o = o_inter + intra[c]
        st = st * decay[c] + kv[c]
        if c + 1 < n_chunks:
            o_inter = _dot_nt(qd[c + 1], st.astype(BF16))
        on = _rms(o, gg_ref[...])
        o_ref[cs, :] = (on * _silu(r_ref[cs, :].astype(F32))).astype(o_ref.dtype)
    st_ref[...] = st

    @pl.when(tb == pl.num_programs(2) - 1)
    def _():
        s_ref[...] = st_ref[...].T


def _gla(q, k, v, r, a, wg_hi, wg_lo, b_gate, g_gla, state, layer_i, n_heads, states_out,
         tb_pref=512):
    b, l, wq = q.shape
    dk = wq // n_heads
    dv = v.shape[2] // n_heads
    ra = a.shape[2]
    lc = min(GLA_CHUNK, l)
    tb = min(tb_pref, l)
    assert l % tb == 0 and tb % lc == 0 and lc % (2 * SUBLANES) == 0
    qk_spec = pl.BlockSpec((None, tb, dk), lambda bi, h, t: (bi, t, h))
    vr_spec = pl.BlockSpec((None, tb, dv), lambda bi, h, t: (bi, t, h))
    in_specs = [
        qk_spec, qk_spec, vr_spec, vr_spec,
        pl.BlockSpec((None, tb, ra), lambda bi, h, t: (bi, t, 0)),
        pl.BlockSpec((None, ra, dk), lambda bi, h, t: (layer_i, 0, h)),
        pl.BlockSpec((None, ra, dk), lambda bi, h, t: (layer_i, 0, h)),
        pl.BlockSpec((None, 1, dk), lambda bi, h, t: (layer_i, 0, h)),
        pl.BlockSpec((None, 1, dv), lambda bi, h, t: (layer_i, 0, 0)),
    ]
    args = [q, k, v, r, a, wg_hi, wg_lo, b_gate.reshape(-1, 1, wq), g_gla.reshape(-1, 1, dv)]
    has_state = state is not None
    if has_state:
        in_specs.append(
            pl.BlockSpec((None, None, None, dk, dv), lambda bi, h, t: (layer_i, bi, h, 0, 0)))
        args.append(state)
    aliases = {}
    if states_out is not None:
        aliases[len(args)] = 1
        in_specs.append(pl.BlockSpec(memory_space=pl.ANY))
        args.append(states_out)
    n_layers = b_gate.shape[0]
    return pl.pallas_call(
        functools.partial(_gla_kernel, lc=lc, has_state=has_state,
                          has_carried=states_out is not None, q_scale=dk ** -0.5),
        out_shape=[jax.ShapeDtypeStruct((b, l, n_heads * dv), BF16),
                   jax.ShapeDtypeStruct((n_layers, b, n_heads, dk, dv), F32)],
        grid=(b, n_heads, l // tb),
        in_specs=in_specs,
        out_specs=[vr_spec, pl.BlockSpec((None, None, None, dk, dv),
                                         lambda bi, h, t: (layer_i, bi, h, 0, 0))],
        scratch_shapes=[pltpu.VMEM((dv, dk), F32), pltpu.VMEM((tb, dk), F32)],
        input_output_aliases=aliases,
        compiler_params=_params(3),
        name="gated_linear_attention",
    )(*args)


def _trunk(x3, grp, mod, weights, dims, cache_k, cache_v, state_gla):
    (g_norm, w_in_even, w_out_even, g_sgu, sgu_mix, sgu_bias, w_in_odd, w_a, wg, b_gate, g_gla,
     w_out_odd, w_ff_up, w_ff_down) = weights
    sb_heads, sb_width, sgu_width, gla_heads, gla_qk, gla_vw = dims
    n_seq, seq_len, d = x3.shape
    sample = cache_k is not None
    x = x3.reshape(n_seq * seq_len, d)
    seq = lambda t: t.reshape(n_seq, seq_len, t.shape[-1])
    depth = g_norm.shape[0]
    n_even = (depth + 1) // 2
    k_all = v_all = None
    states = jnp.zeros((depth // 2, n_seq, gla_heads, gla_qk // gla_heads, gla_vw // gla_heads), F32)
    gvs = []
    for layer in range(depth):
        i = layer // 2
        if layer % 2 == 0:
            sb_scale = (sb_width // sb_heads) ** -0.5
            q, k_all, v_all, u, gv = _inproj(
                x, mod, layer, g_norm[layer, 0], w_in_even, i,
                [(sb_width, BF16, sb_scale, None), (sb_width, F32, 1.0, (n_even, k_all)),
                 (sb_width, F32, 1.0, (n_even, v_all)), (sgu_width, BF16, 1.0, None),
                 (sgu_width, F32, 1.0, None)], grp)
            kv_seq = lambda t: t.reshape(n_even, n_seq, seq_len, sb_width)
            if sample:
                o_a = _sb_sample(seq(q), kv_seq(k_all), kv_seq(v_all), cache_k, cache_v, i)
            else:
                o_a = _sb_prompt(seq(q), kv_seq(k_all), kv_seq(v_all), i, sb_heads)
            o_b, vn = _sgu(u, gv, g_sgu[i], sgu_mix, sgu_bias, i, emit_vn=sample)
            x = _outproj([o_a.reshape(x.shape[0], sb_width), o_b], w_out_even, i, x, mod, layer,
                         g_norm[layer, 1], grp)
            gvs.append(vn)
        else:
            q, k, v, r, a = _inproj(
                x, mod, layer, g_norm[layer, 0], w_in_odd, i,
                [(gla_qk, BF16, 1.0, None), (gla_qk, BF16, 1.0, None), (gla_vw, BF16, 1.0, None),
                 (gla_vw, BF16, 1.0, None)], grp, aux_w=w_a)
            o, states = _gla(seq(q), seq(k), seq(v), seq(r), seq(a), wg[0], wg[1], b_gate, g_gla,
                             state_gla if sample else None, i, gla_heads, states)
            x = _outproj([o.reshape(x.shape[0], gla_vw)], w_out_odd, i, x, mod, layer,
                         g_norm[layer, 1], grp)
        x = _ff(x, mod, layer, g_norm[layer, 2], g_norm[layer, 3], w_ff_up, w_ff_down, grp)
    return x.reshape(n_seq, seq_len, d), k_all, v_all, states, gvs


def kernel(x_prompt, x_sample, cache_sb_k, cache_sb_v, state_gla, c_prompt, c_sample, w_mod, b_mod, g_norm, w_in_even, w_out_even, g_sgu, w_sgu, b_sgu, w_in_odd, w_gate_up, b_gate, g_gla, w_out_odd, w_ff_up, w_ff_down):
    batch, seq, d = x_prompt.shape
    dec_batch, dec_seq, _ = x_sample.shape
    n_even, _, past, sb_heads, sb_dh = cache_sb_k.shape
    sb_width = sb_heads * sb_dh
    _, _, gla_heads, gla_dk, gla_dv = state_gla.shape
    gla_qk, gla_vw = gla_heads * gla_dk, gla_heads * gla_dv
    _, groups, sgu_chunk, _ = w_sgu.shape
    sgu_width = g_sgu.shape[1]
    rank = w_gate_up.shape[1]
    depth, d_ff = w_ff_up.shape[0], w_ff_up.shape[2]
    dims = (sb_heads, sb_width, sgu_width, gla_heads, gla_qk, gla_vw)

    mod = _modulation(jnp.concatenate([c_sample, c_prompt], axis=0), w_mod, b_mod)
    mod = mod.reshape(mod.shape[0], mod.shape[1], 1, mod.shape[2])

    cast = lambda t: t.astype(BF16)
    n_main = 2 * gla_qk + 2 * gla_vw
    w_a = jnp.pad(w_in_odd[:, :, n_main:], ((0, 0), (0, 0), (0, LANES - rank)))
    w_a_hi = cast(w_a)
    w_a = (w_a_hi, cast(w_a - w_a_hi.astype(F32)))
    wg = jnp.pad(w_gate_up, ((0, 0), (0, LANES - rank), (0, 0)))
    wg_hi = cast(wg)
    wg = (wg_hi, cast(wg - wg_hi.astype(F32)))
    w_tri = w_sgu * jnp.tril(jnp.ones((sgu_chunk, sgu_chunk), F32))

    def sgu_tables(chunk_len):
        mix = cast(w_tri[:, :, :chunk_len, :chunk_len])
        bias = jnp.repeat(jnp.swapaxes(b_sgu[:, :, :chunk_len], 1, 2), sgu_width // groups, axis=2)
        return mix, bias

    common = (g_norm, cast(w_in_even), cast(w_out_even), g_sgu)
    tail = (cast(w_in_odd), w_a, wg, b_gate, g_gla, cast(w_out_odd), cast(w_ff_up), cast(w_ff_down))

    grp_p = _Group(batch, seq, dec_batch, 512)
    grp_s = _Group(dec_batch, dec_seq, 0, 512)

    y_p, ks_p, vs_p, st_p, _ = _trunk(
        x_prompt, grp_p, mod, common + sgu_tables(sgu_chunk) + tail, dims, None, None, None)
    y_s, ks_s, vs_s, st_s, gv_s = _trunk(
        x_sample, grp_s, mod, common + sgu_tables(dec_seq) + tail, dims,
        cache_sb_k, cache_sb_v, state_gla)

    heads = lambda t, b, l: t.reshape(t.shape[0], b, l, sb_heads, sb_dh)
    return (y_p, y_s,
            heads(ks_p, batch, seq), heads(vs_p, batch, seq), st_p,
            heads(ks_s, dec_batch, dec_seq), heads(vs_s, dec_batch, dec_seq), st_s,
            jnp.stack(gv_s).reshape(len(gv_s), dec_batch, dec_seq, sgu_width))
```

```python
import functools

import jax
import jax.numpy as jnp
from jax import lax
from jax.experimental import pallas as pl
from jax.experimental.pallas import tpu as pltpu

F32 = jnp.float32
BF16 = jnp.bfloat16

EPS = 1e-6
GLA_CHUNK = 64
GLA_TAU = 16.0
SUBLANES = 8
GLA_DIRECT = 2
LANES = 128
MASKED_LOG = -1e30
VMEM_LIMIT_BYTES = 56 * 1024 * 1024


def _params(n_grid):
    return pltpu.CompilerParams(
        dimension_semantics=("arbitrary",) * n_grid,
        vmem_limit_bytes=VMEM_LIMIT_BYTES)


def _dot(a, b):
    return jnp.dot(a, b, preferred_element_type=F32)


def _dot_nt(a, b):
    return lax.dot_general(a, b, (((1,), (1,)), ((), ())), preferred_element_type=F32)


def _dot_tn(a, b):
    return lax.dot_general(a, b, (((0,), (0,)), ((), ())), preferred_element_type=F32)


def _split_bf16(x):
    hi = x.astype(BF16)
    lo = (x - hi.astype(F32)).astype(BF16)
    return hi, lo


def _rms(x, g):
    return x * lax.rsqrt(jnp.mean(x * x, axis=-1, keepdims=True) + EPS) * g


def _softplus(z):
    return jnp.maximum(z, 0.0) + jnp.log(1.0 + jnp.exp(-jnp.abs(z)))


def _silu(x):
    return x * jax.nn.sigmoid(x)


def _rows(ref, rows):
    v = ref[...]
    nb, _, d = v.shape
    if nb == 1:
        return v[0]
    return jnp.broadcast_to(v, (nb, rows, d)).reshape(nb * rows, d)


class _Group:
    def __init__(self, n_seq, seq_len, mod_row0, tm_pref):
        self.n_seq, self.seq_len, self.mod_row0 = n_seq, seq_len, mod_row0
        self.n = n_seq * seq_len
        if seq_len >= tm_pref:
            assert seq_len % tm_pref == 0
            self.tm, self.nb = tm_pref, 1
        else:
            self.nb = min(n_seq, tm_pref // seq_len)
            assert n_seq % self.nb == 0
            self.tm = self.nb * seq_len
        self.rows = self.tm // self.nb
        self.blocks_per_seq = max(1, seq_len // self.tm)

    def mod_spec(self, layer, col, d):
        nb, bps, row0 = self.nb, self.blocks_per_seq, self.mod_row0

        def index(i, *_):
            if nb == 1:
                return (layer, row0 + i // bps, 0, col)
            return (layer, row0 // nb + i, 0, col)

        return pl.BlockSpec((None, nb, 1, d), index)


def _mod_kernel(c_ref, w_ref, b_ref, o_ref):
    cs = _silu(c_ref[...]).astype(BF16)
    o_ref[...] = _dot(cs, w_ref[...].astype(BF16)) + b_ref[...]


def _modulation(c_all, w_mod, b_mod):
    depth, d, n_out = w_mod.shape
    n_c = c_all.shape[0]
    tn = min(n_out, 1024)
    return pl.pallas_call(
        _mod_kernel,
        out_shape=jax.ShapeDtypeStruct((depth, n_c, n_out), F32),
        grid=(depth, n_out // tn),
        in_specs=[
            pl.BlockSpec((n_c, d), lambda l, j: (0, 0)),
            pl.BlockSpec((None, d, tn), lambda l, j: (l, 0, j)),
            pl.BlockSpec((None, 1, tn), lambda l, j: (l, 0, j)),
        ],
        out_specs=pl.BlockSpec((None, n_c, tn), lambda l, j: (l, 0, j)),
        compiler_params=_params(2),
        name="modulation",
    )(c_all, w_mod, b_mod.reshape(depth, 1, n_out))


def _inproj_kernel(*refs, seg_cols, seg_scales, rows, has_aux, n_carried, n_slabs, layer_i):
    x_ref, g_ref, sh_ref, sc_ref, w_ref = refs[:5]
    pos = 5
    if has_aux:
        wah_ref, wal_ref = refs[5:7]
        pos = 7
    pos += n_carried
    out_refs = refs[pos:pos + len(seg_cols)]
    if has_aux:
        aux_ref = refs[pos + len(seg_cols)]
    tm = x_ref.shape[0]
    scale_rows, shift_rows = _rows(sc_ref, rows), _rows(sh_ref, rows)
    for s in range(n_slabs):
        rs = slice(s * tm // n_slabs, (s + 1) * tm // n_slabs)
        per_row = lambda v: v if v.shape[0] == 1 else v[rs]
        h = _rms(x_ref[rs, :], g_ref[...]) * (1.0 + per_row(scale_rows)) + per_row(shift_rows)
        h_hi, h_lo = _split_bf16(h)
        if has_aux:
            aux_ref[rs, :] = (_dot(h_hi, wah_ref[...]) + _dot(h_lo, wah_ref[...])
                              + _dot(h_hi, wal_ref[...]))
        for o_ref, (c0, c1), scale in zip(out_refs, seg_cols, seg_scales):
            y = _dot(h_hi, w_ref[:, c0:c1])
            if scale != 1.0:
                y = y * scale
            if len(o_ref.shape) == 2:
                o_ref[rs, :] = y.astype(o_ref.dtype)
            else:
                for l in range(o_ref.shape[0]):
                    o_ref[l, rs, :] = (y if l == layer_i else jnp.zeros_like(y)).astype(o_ref.dtype)


def _inproj(x, mod, layer, g, w, layer_i, segs, grp, aux_w=None):
    n, d = x.shape
    tm = grp.tm
    w_cols = w.shape[2]
    in_specs = [
        pl.BlockSpec((tm, d), lambda i: (i, 0)),
        pl.BlockSpec((1, d), lambda i: (0, 0)),
        grp.mod_spec(layer, 0, d),
        grp.mod_spec(layer, 1, d),
        pl.BlockSpec((None, d, w_cols), lambda i: (layer_i, 0, 0), pipeline_mode=pl.Buffered(1)),
    ]
    args = [x, g.reshape(1, d), mod, mod, w]
    has_aux = aux_w is not None
    if has_aux:
        wah, wal = aux_w
        wa = wah.shape[2]
        in_specs += [pl.BlockSpec((None, d, wa), lambda i: (layer_i, 0, 0))] * 2
        args += [wah, wal]
    out_shape, out_specs, aliases, seg_cols = [], [], {}, []
    c0 = 0
    for k, (wd, dt, _, stack) in enumerate(segs):
        seg_cols.append((c0, c0 + wd))
        c0 += wd
        if stack is None:
            out_shape.append(jax.ShapeDtypeStruct((n, wd), dt))
            out_specs.append(pl.BlockSpec((tm, wd), lambda i: (i, 0)))
        else:
            n_layers, previous = stack
            out_shape.append(jax.ShapeDtypeStruct((n_layers, n, wd), dt))
            if previous is None:
                out_specs.append(pl.BlockSpec((n_layers, tm, wd), lambda i: (0, i, 0)))
            else:
                out_specs.append(pl.BlockSpec((None, tm, wd), lambda i: (layer_i, i, 0)))
                aliases[len(args)] = k
                in_specs.append(pl.BlockSpec(memory_space=pl.ANY))
                args.append(previous)
    assert c0 <= w_cols and w.shape[1] == d
    if has_aux:
        out_shape.append(jax.ShapeDtypeStruct((n, wa), F32))
        out_specs.append(pl.BlockSpec((tm, wa), lambda i: (i, 0)))
    return pl.pallas_call(
        functools.partial(_inproj_kernel, seg_cols=tuple(seg_cols),
                          seg_scales=tuple(sg[2] for sg in segs), rows=grp.rows, has_aux=has_aux,
                          n_carried=len(aliases), n_slabs=2 if tm % 32 == 0 else 1,
                          layer_i=layer_i),
        out_shape=out_shape,
        grid=(n // tm,),
        in_specs=in_specs,
        out_specs=out_specs,
        input_output_aliases=aliases,
        compiler_params=_params(1),
        name="in_proj",
    )(*args)


def _outproj_kernel(*refs, n_a, rows):
    a_refs = refs[:n_a]
    w_refs = refs[n_a:2 * n_a]
    x_ref, g_ref, gt_ref, o_ref = refs[2 * n_a:]
    tm = x_ref.shape[0]
    n_slabs = 2 if tm % 32 == 0 else 1
    gate_rows = _rows(gt_ref, rows)
    for s in range(n_slabs):
        rs = slice(s * tm // n_slabs, (s + 1) * tm // n_slabs)
        y = _dot(a_refs[0][rs, :], w_refs[0][...])
        for a_ref, w_ref in zip(a_refs[1:], w_refs[1:]):
            y = y + _dot(a_ref[rs, :], w_ref[...])
        gate = gate_rows if gate_rows.shape[0] == 1 else gate_rows[rs]
        o_ref[rs, :] = x_ref[rs, :] + gate * _rms(y, g_ref[...])


def _outproj(acts, w, layer_i, x, mod, layer, g, grp):
    n, d = x.shape
    tm = grp.tm
    ka = acts[0].shape[1]
    assert all(a.shape[1] == ka for a in acts) and w.shape[1] == ka * len(acts)
    in_specs = [pl.BlockSpec((tm, ka), lambda i: (i, 0)) for _ in acts]
    in_specs += [pl.BlockSpec((None, ka, d), lambda i, r=r: (layer_i, r, 0)) for r in range(len(acts))]
    in_specs += [
        pl.BlockSpec((tm, d), lambda i: (i, 0)),
        pl.BlockSpec((1, d), lambda i: (0, 0)),
        grp.mod_spec(layer, 2, d),
    ]
    return pl.pallas_call(
        functools.partial(_outproj_kernel, n_a=len(acts), rows=grp.rows),
        out_shape=jax.ShapeDtypeStruct((n, d), F32),
        grid=(n // tm,),
        in_specs=in_specs,
        out_specs=pl.BlockSpec((tm, d), lambda i: (i, 0)),
        compiler_params=_params(1),
        name="out_proj",
    )(*acts, *([w] * len(acts)), x, g.reshape(1, d), mod)


def _ff_kernel(x_ref, g1_ref, sh_ref, sc_ref, wu_ref, wd_ref, g2_ref, gt_ref, o_ref,
               h_ref, t_ref, y_ref, *, rows, n_a, n_b):
    j = pl.program_id(1)
    tf = t_ref.shape[2]
    tn = y_ref.shape[2]

    def hidden(h):
        t = jnp.maximum(_dot(h, wu_ref[...]), 0.0)
        return (t * t).astype(BF16)

    @pl.when(j == 0)
    def _():
        tm = x_ref.shape[0]
        n_slabs = 2 if tm % 32 == 0 else 1
        scale_rows, shift_rows = _rows(sc_ref, rows), _rows(sh_ref, rows)
        for s in range(n_slabs):
            rs = slice(s * tm // n_slabs, (s + 1) * tm // n_slabs)
            per_row = lambda v: v if v.shape[0] == 1 else v[rs]
            h = _rms(x_ref[rs, :], g1_ref[...]) * (1.0 + per_row(scale_rows)) + per_row(shift_rows)
            h = h.astype(BF16)
            h_ref[rs, :] = h
            t_ref[0, rs, :] = hidden(h)

    @pl.when((j > 0) & (j < n_a))
    def _():
        t_ref[j] = hidden(h_ref[...])

    @pl.when(j >= n_a)
    def _():
        acc = _dot(t_ref[0], wd_ref[0:tf, :])
        for a in range(1, n_a):
            acc = acc + _dot(t_ref[a], wd_ref[a * tf:(a + 1) * tf, :])
        y_ref[j - n_a] = acc

    @pl.when(j == n_a + n_b - 1)
    def _():
        ss = jnp.sum(jnp.square(y_ref[0]), axis=-1, keepdims=True)
        for b in range(1, n_b):
            ss = ss + jnp.sum(jnp.square(y_ref[b]), axis=-1, keepdims=True)
        rs = lax.rsqrt(ss * (1.0 / (n_b * tn)) + EPS)
        gate = _rows(gt_ref, rows)
        g2 = g2_ref[...]
        for b in range(n_b):
            cs = slice(b * tn, (b + 1) * tn)
            o_ref[:, cs] = x_ref[:, cs] + gate[:, cs] * (y_ref[b] * rs * g2[:, cs])


def _ff(x, mod, layer, g1, g2, w_up, w_down, grp):
    n, d = x.shape
    f = w_up.shape[2]
    tm = grp.tm
    tf = min(1024, f)
    tn = min(4 * LANES, d)
    assert f % tf == 0 and d % tn == 0
    n_a, n_b = f // tf, d // tn
    return pl.pallas_call(
        functools.partial(_ff_kernel, rows=grp.rows, n_a=n_a, n_b=n_b),
        out_shape=jax.ShapeDtypeStruct((n, d), F32),
        grid=(n // tm, n_a + n_b),
        in_specs=[
            pl.BlockSpec((tm, d), lambda i, j: (i, 0)),
            pl.BlockSpec((1, d), lambda i, j: (0, 0)),
            grp.mod_spec(layer, 3, d),
            grp.mod_spec(layer, 4, d),
            pl.BlockSpec((None, d, tf), lambda i, j: (layer, 0, jnp.minimum(j, n_a - 1))),
            pl.BlockSpec((None, f, tn), lambda i, j: (layer, 0, jnp.clip(j - n_a, 0, n_b - 1))),
            pl.BlockSpec((1, d), lambda i, j: (0, 0)),
            grp.mod_spec(layer, 5, d),
        ],
        out_specs=pl.BlockSpec((tm, d), lambda i, j: (i, 0)),
        scratch_shapes=[pltpu.VMEM((tm, d), BF16), pltpu.VMEM((n_a, tm, tf), BF16),
                        pltpu.VMEM((n_b, tm, tn), F32)],
        compiler_params=_params(2),
        name="channel_mlp",
    )(x, g1.reshape(1, d), mod, mod, w_up, w_down, g2.reshape(1, d), mod)


def _strict_upper(n):
    r = lax.broadcasted_iota(jnp.int32, (n, n), 0)
    c = lax.broadcasted_iota(jnp.int32, (n, n), 1)
    return jnp.where(r > c, 1.0, 0.0).astype(BF16)


def _sb_prompt_kernel(q_ref, k_ref, v_ref, o_ref, kb_ref, vb_ref, lb_ref, hi_ref, lo_ref,
                      acc_ref, carry_ref, sum_ref, *, t, dh):
    qi = pl.program_id(2)
    heads = range(q_ref.shape[1] // dh)
    cols = [slice(h * dh, (h + 1) * dh) for h in heads]

    @pl.when(qi == 0)
    def _():
        kb_ref[...] = k_ref[...].astype(BF16)
        vb_ref[...] = v_ref[...].astype(BF16)

    u = _strict_upper(t)
    qs = [q_ref[:, cs] for cs in cols]

    def logits(start):
        return [_dot_nt(q, kb_ref[pl.ds(start, t), cs]) for q, cs in zip(qs, cols)]

    def stage1(zs, slot, diagonal):
        sps = [_softplus(z) for z in zs]
        if diagonal:
            r = lax.broadcasted_iota(jnp.int32, (t, t), 0)
            c = lax.broadcasted_iota(jnp.int32, (t, t), 1)
            negs = [jnp.where(c < r, -sp, 0.0) for sp in sps]
            log_betas = [jnp.where(c < r, z - sp, MASKED_LOG) for z, sp in zip(zs, sps)]
        else:
            negs = [-sp for sp in sps]
            log_betas = [z - sp for z, sp in zip(zs, sps)]
        for h in heads:
            lb_ref[slot, h] = log_betas[h]
            hi_ref[slot, h], lo_ref[slot, h] = _split_bf16(negs[h])
            sum_ref[h] = jnp.sum(negs[h], axis=-1, keepdims=True)

    def suffixes(slot):
        return [_dot(hi_ref[slot, h], u) + _dot(lo_ref[slot, h], u) for h in heads]

    def stage2(sufs, start, slot):
        ws = [jnp.exp(lb_ref[slot, h] + sufs[h] + carry_ref[h]).astype(BF16) for h in heads]
        for h in heads:
            acc_ref[h] += _dot(ws[h], vb_ref[pl.ds(start, t), cols[h]])

    acc_ref[...] = jnp.zeros_like(acc_ref)
    carry_ref[...] = jnp.zeros_like(carry_ref)
    stage1(logits(pl.multiple_of(qi * t, t)), 0, True)

    def body(it, _):
        slot = it % 2
        sufs = suffixes(slot)
        zs = logits(pl.multiple_of((qi - 1 - it) * t, t))
        stage2(sufs, pl.multiple_of((qi - it) * t, t), slot)
        for h in heads:
            carry_ref[h] += sum_ref[h]
        stage1(zs, 1 - slot, False)
        return 0

    lax.fori_loop(0, qi, body, 0)
    stage2(suffixes(qi % 2), 0, qi % 2)
    for h in heads:
        o_ref[:, cols[h]] = acc_ref[h].astype(o_ref.dtype)


def _sb_prompt(q, k, v, layer_i, n_heads, t_pref=256, heads_per_step=4):
    b, l, w = q.shape
    dh = w // n_heads
    t = min(t_pref, l)
    hs = min(heads_per_step, n_heads)
    assert l % t == 0 and n_heads % hs == 0
    q_spec = pl.BlockSpec((None, t, hs * dh), lambda bi, h, qi: (bi, qi, h))
    kv_spec = pl.BlockSpec((None, None, l, hs * dh), lambda bi, h, qi: (layer_i, bi, 0, h))
    return pl.pallas_call(
        functools.partial(_sb_prompt_kernel, t=t, dh=dh),
        out_shape=jax.ShapeDtypeStruct((b, l, w), BF16),
        grid=(b, n_heads // hs, l // t),
        in_specs=[q_spec, kv_spec, kv_spec],
        out_specs=q_spec,
        scratch_shapes=[pltpu.VMEM((l, hs * dh), BF16), pltpu.VMEM((l, hs * dh), BF16),
                        pltpu.VMEM((2, hs, t, t), F32), pltpu.VMEM((2, hs, t, t), BF16),
                        pltpu.VMEM((2, hs, t, t), BF16), pltpu.VMEM((hs, t, dh), F32),
                        pltpu.VMEM((hs, t, 1), F32), pltpu.VMEM((hs, t, 1), F32)],
        compiler_params=_params(3),
        name="stick_breaking_prompt",
    )(q, k, v)


def _sb_sample_kernel(q_ref, kn_ref, vn_ref, kc_ref, vc_ref, o_ref, *, tk, n_heads):
    lq, w = q_ref.shape
    dh = kc_ref.shape[1]
    past = kc_ref.shape[0] // n_heads
    nl = n_heads * lq

    qf = q_ref[...].astype(F32)
    row_h = lax.broadcasted_iota(jnp.int32, (nl, w), 0) // lq
    col_h = lax.broadcasted_iota(jnp.int32, (nl, w), 1) // dh
    q_exp = jnp.where(row_h == col_h, jnp.concatenate([qf] * n_heads, axis=0), 0.0).astype(BF16)

    def strict_lower_t(n):
        r = lax.broadcasted_iota(jnp.int32, (n, n), 0)
        c = lax.broadcasted_iota(jnp.int32, (n, n), 1)
        return jnp.where(c > r, 1.0, 0.0).astype(BF16)

    def block(k_heads, v_heads, carry, accs, new):
        n = k_heads[0].shape[0]
        z = _dot_nt(k_heads[0], q_exp[:, 0:dh])
        for h in range(1, n_heads):
            z = z + _dot_nt(k_heads[h], q_exp[:, h * dh:(h + 1) * dh])
        sp = _softplus(z)
        if new:
            key = lax.broadcasted_iota(jnp.int32, (n, nl), 0)
            qry = lax.broadcasted_iota(jnp.int32, (n, nl), 1) % lq
            mask = key < qry
            neg = jnp.where(mask, -sp, 0.0)
        else:
            neg = -sp
        hi, lo = _split_bf16(neg)
        ut = strict_lower_t(n)
        suffix = _dot(ut, hi) + _dot(ut, lo)
        wgt = jnp.exp(z - sp + suffix + carry)
        if new:
            wgt = jnp.where(mask, wgt, 0.0)
        wt = wgt.T.astype(BF16)
        accs = [acc + _dot(wt[h * lq:(h + 1) * lq, :], v_heads[h]) for h, acc in enumerate(accs)]
        carry = carry + jnp.sum(neg, axis=0, keepdims=True)
        return carry, accs

    pad = jnp.zeros((LANES - lq, w), F32)
    kn = jnp.concatenate([kn_ref[...], pad], axis=0).astype(BF16)
    vn = jnp.concatenate([vn_ref[...], pad], axis=0).astype(BF16)
    heads = lambda a: [a[:, h * dh:(h + 1) * dh] for h in range(n_heads)]
    carry, accs = block(heads(kn), heads(vn), jnp.zeros((1, nl), F32),
                        [jnp.zeros((lq, dh), F32)] * n_heads, True)
    for i in reversed(range(past // tk)):
        rows = [pl.ds(i * tk * n_heads + h, tk, stride=n_heads) for h in range(n_heads)]
        ks = [kc_ref[r, :].astype(BF16) for r in rows]
        vs = [vc_ref[r, :].astype(BF16) for r in rows]
        carry, accs = block(ks, vs, carry, accs, False)

    o_ref[...] = jnp.concatenate(accs, axis=1).astype(o_ref.dtype)


def _sb_sample(q, k_new, v_new, cache_k, cache_v, layer_i, tk_pref=256):
    b, lq, w = q.shape
    _, _, past, n_heads, dh = cache_k.shape
    tk = min(tk_pref, past)
    assert past % tk == 0 and lq <= LANES
    new_spec = pl.BlockSpec((None, lq, w), lambda bi: (bi, 0, 0))
    kv_new_spec = pl.BlockSpec((None, None, lq, w), lambda bi: (layer_i, bi, 0, 0))
    cache_spec = pl.BlockSpec((None, None, past * n_heads, dh), lambda bi: (layer_i, bi, 0, 0))
    flat = lambda c: c.reshape(c.shape[0], c.shape[1], past * n_heads, dh)
    return pl.pallas_call(
        functools.partial(_sb_sample_kernel, tk=tk, n_heads=n_heads),
        out_shape=jax.ShapeDtypeStruct((b, lq, w), BF16),
        grid=(b,),
        in_specs=[new_spec, kv_new_spec, kv_new_spec, cache_spec, cache_spec],
        out_specs=new_spec,
        compiler_params=_params(1),
        name="stick_breaking_sample",
    )(q, k_new, v_new, flat(cache_k), flat(cache_v))


def _sgu_kernel(u_ref, gv_ref, g_ref, w_ref, b_ref, o_ref, *rest, r, gd):
    vb_ref = rest[-1]
    rb, width = u_ref.shape
    vn = _rms(jax.nn.gelu(gv_ref[...]), g_ref[...])
    if len(rest) == 2:
        rest[0][...] = vn
    vb_ref[...] = vn.astype(BF16)

    def chunk(s, _):
        rs = pl.ds(pl.multiple_of(s * r, r), r)
        for gi in range(width // gd):
            cs = slice(gi * gd, (gi + 1) * gd)
            mix = _dot(w_ref[gi], vb_ref[rs, cs]) + b_ref[:, cs]
            o_ref[rs, cs] = (jax.nn.gelu(u_ref[rs, cs].astype(F32)) * mix).astype(o_ref.dtype)
        return 0

    lax.fori_loop(0, rb // r, chunk, 0)


def _sgu(u, gv, g_sgu, w_mix, bias, layer_i, emit_vn, rb_pref=512):
    n, width = u.shape
    _, groups, r, _ = w_mix.shape
    rb = max(r, min(rb_pref, n))
    assert n % rb == 0 and rb % r == 0
    row_spec = pl.BlockSpec((rb, width), lambda i: (i, 0))
    out_shape = [jax.ShapeDtypeStruct((n, width), BF16)]
    out_specs = [row_spec]
    if emit_vn:
        out_shape.append(jax.ShapeDtypeStruct((n, width), F32))
        out_specs.append(row_spec)
    res = pl.pallas_call(
        functools.partial(_sgu_kernel, r=r, gd=width // groups),
        out_shape=out_shape,
        grid=(n // rb,),
        in_specs=[
            row_spec, row_spec,
            pl.BlockSpec((1, width), lambda i: (0, 0)),
            pl.BlockSpec((None, groups, r, r), lambda i: (layer_i, 0, 0, 0)),
            pl.BlockSpec((None, r, width), lambda i: (layer_i, 0, 0)),
        ],
        out_specs=out_specs,
        scratch_shapes=[pltpu.VMEM((rb, width), BF16)],
        compiler_params=_params(1),
        name="spatial_gating",
    )(u, gv, g_sgu.reshape(1, width), w_mix, bias)
    return res if emit_vn else (res[0], None)


def _gla_kernel(*refs, lc, has_state, has_carried, q_scale):
    q_ref, k_ref, v_ref, r_ref, a_ref, wgh_ref, wgl_ref, bg_ref, gg_ref = refs[:9]
    s0_ref = refs[9] if has_state else None
    o_ref, s_ref, st_ref, lg_ref = refs[9 + has_state + has_carried:]
    tb = pl.program_id(2)
    n_rows, dk = q_ref.shape
    n_chunks = n_rows // lc

    @pl.when(tb == 0)
    def _():
        if has_state:
            st_ref[...] = s0_ref[...].T
        else:
            st_ref[...] = jnp.zeros_like(st_ref)

    a_hi, a_lo = _split_bf16(a_ref[...])
    pre = (_dot(a_hi, wgh_ref[...]) + _dot(a_lo, wgh_ref[...]) + _dot(a_hi, wgl_ref[...])
           + bg_ref[...])
    lg_ref[...] = -_softplus(-pre) * (1.0 / GLA_TAU)

    row = lax.broadcasted_iota(jnp.int32, (lc, lc), 0)
    col = lax.broadcasted_iota(jnp.int32, (lc, lc), 1)
    lower = jnp.where(col <= row, 1.0, 0.0).astype(BF16)

    chunks = [slice(c * lc, (c + 1) * lc) for c in range(n_chunks)]
    per_chunk = lambda x: [x[cs] for cs in chunks]

    def roll_in_group(x, d):
        x3 = x.reshape(n_rows // SUBLANES, SUBLANES, x.shape[1])
        return pltpu.roll(x3, d, 1).reshape(x.shape)

    lg_hi, lg_lo = _split_bf16(lg_ref[...])
    b = jnp.concatenate(
        [_dot(lower, hi) + _dot(lower, lo) for hi, lo in zip(per_chunk(lg_hi), per_chunk(lg_lo))],
        axis=0)
    q = q_ref[...].astype(F32) * q_scale
    k = k_ref[...].astype(F32)

    scores = [jnp.zeros((lc, lc), F32)] * n_chunks
    h = lc // 2
    while h >= GLA_DIRECT:
        b3 = b.reshape(n_rows // (2 * h), 2 * h, dk)
        f = jnp.exp(-jnp.abs(b3 - b3[:, h - 1:h, :])).reshape(n_rows, dk)
        qf, kf = (q * f).astype(BF16), (k * f).astype(BF16)
        mask = ((row // h) == (col // h) + 1) & ((col // h) % 2 == 0)
        scores = [sc + jnp.where(mask, _dot_nt(qc, kc), 0.0)
                  for sc, qc, kc in zip(scores, per_chunk(qf), per_chunk(kf))]
        h //= 2
    for d in range(GLA_DIRECT):
        if d == 0:
            pd = jnp.sum(q * k, axis=-1, keepdims=True)
        else:
            kd = roll_in_group(k, d)
            bd = roll_in_group(b, d)
            pd = jnp.sum(q * kd * jnp.exp(jnp.minimum(b - bd, 0.0)), axis=-1, keepdims=True)
        mask = (col == row - d) & ((row % GLA_DIRECT) >= d)
        scores = [sc + jnp.where(mask, pc, 0.0) for sc, pc in zip(scores, per_chunk(pd))]

    b_last = jnp.broadcast_to(b.reshape(n_chunks, lc, dk)[:, lc - 1:lc, :], (n_chunks, lc, dk))
    decay = [jnp.exp(bl[0:1, :]) for bl in per_chunk(b_last.reshape(n_rows, dk))]
    qd = per_chunk((q * jnp.exp(b)).astype(BF16))
    k_dec = per_chunk((k * jnp.exp(b_last.reshape(n_rows, dk) - b)).astype(BF16))
    vs = [v_ref[cs, :] for cs in chunks]
    intra = [_dot(sc.astype(BF16), vc) for sc, vc in zip(scores, vs)]
    kv = [_dot_tn(vc, kc) for vc, kc in zip(vs, k_dec)]

    st = st_ref[...]
    o_inter = _dot_nt(qd[0], st.astype(BF16))
    for c, cs in enumerate(chunks):
        o = o_inter + intra[c]
        st = st * decay[c] + kv[c]
        if c + 1 < n_chunks:
            o_inter = _dot_nt(qd[c + 1], st.astype(BF16))
        on = _rms(o, gg_ref[...])
        o_ref[cs, :] = (on * _silu(r_ref[cs, :].astype(F32))).astype(o_ref.dtype)
    st_ref[...] = st

    @pl.when(tb == pl.num_programs(2) - 1)
    def _():
        s_ref[...] = st_ref[...].T


def _gla(q, k, v, r, a, wg_hi, wg_lo, b_gate, g_gla, state, layer_i, n_heads, states_out,
         tb_pref=512):
    b, l, wq = q.shape
    dk = wq // n_heads
    dv = v.shape[2] // n_heads
    ra = a.shape[2]
    lc = min(GLA_CHUNK, l)
    tb = min(tb_pref, l)
    assert l % tb == 0 and tb % lc == 0 and lc % (2 * SUBLANES) == 0
    qk_spec = pl.BlockSpec((None, tb, dk), lambda bi, h, t: (bi, t, h))
    vr_spec = pl.BlockSpec((None, tb, dv), lambda bi, h, t: (bi, t, h))
    in_specs = [
        qk_spec, qk_spec, vr_spec, vr_spec,
        pl.BlockSpec((None, tb, ra), lambda bi, h, t: (bi, t, 0)),
        pl.BlockSpec((None, ra, dk), lambda bi, h, t: (layer_i, 0, h)),
        pl.BlockSpec((None, ra, dk), lambda bi, h, t: (layer_i, 0, h)),
        pl.BlockSpec((None, 1, dk), lambda bi, h, t: (layer_i, 0, h)),
        pl.BlockSpec((None, 1, dv), lambda bi, h, t: (layer_i, 0, 0)),
    ]
    args = [q, k, v, r, a, wg_hi, wg_lo, b_gate.reshape(-1, 1, wq), g_gla.reshape(-1, 1, dv)]
    has_state = state is not None
    if has_state:
        in_specs.append(
            pl.BlockSpec((None, None, None, dk, dv), lambda bi, h, t: (layer_i, bi, h, 0, 0)))
        args.append(state)
    aliases = {}
    if states_out is not None:
        aliases[len(args)] = 1
        in_specs.append(pl.BlockSpec(memory_space=pl.ANY))
        args.append(states_out)
    n_layers = b_gate.shape[0]
    return pl.pallas_call(
        functools.partial(_gla_kernel, lc=lc, has_state=has_state,
                          has_carried=states_out is not None, q_scale=dk ** -0.5),
        out_shape=[jax.ShapeDtypeStruct((b, l, n_heads * dv), BF16),
                   jax.ShapeDtypeStruct((n_layers, b, n_heads, dk, dv), F32)],
        grid=(b, n_heads, l // tb),
        in_specs=in_specs,
        out_specs=[vr_spec, pl.BlockSpec((None, None, None, dk, dv),
                                         lambda bi, h, t: (layer_i, bi, h, 0, 0))],
        scratch_shapes=[pltpu.VMEM((dv, dk), F32), pltpu.VMEM((tb, dk), F32)],
        input_output_aliases=aliases,
        compiler_params=_params(3),
        name="gated_linear_attention",
    )(*args)


def _trunk(x3, grp, mod, weights, dims, cache_k, cache_v, state_gla):
    (g_norm, w_in_even, w_out_even, g_sgu, sgu_mix, sgu_bias, w_in_odd, w_a, wg, b_gate, g_gla,
     w_out_odd, w_ff_up, w_ff_down) = weights
    sb_heads, sb_width, sgu_width, gla_heads, gla_qk, gla_vw = dims
    n_seq, seq_len, d = x3.shape
    sample = cache_k is not None
    x = x3.reshape(n_seq * seq_len, d)
    seq = lambda t: t.reshape(n_seq, seq_len, t.shape[-1])
    depth = g_norm.shape[0]
    n_even = (depth + 1) // 2
    k_all = v_all = None
    states = jnp.zeros((depth // 2, n_seq, gla_heads, gla_qk // gla_heads, gla_vw // gla_heads), F32)
    gvs = []
    for layer in range(depth):
        i = layer // 2
        if layer % 2 == 0:
            sb_scale = (sb_width // sb_heads) ** -0.5
            q, k_all, v_all, u, gv = _inproj(
                x, mod, layer, g_norm[layer, 0], w_in_even, i,
                [(sb_width, BF16, sb_scale, None), (sb_width, F32, 1.0, (n_even, k_all)),
                 (sb_width, F32, 1.0, (n_even, v_all)), (sgu_width, BF16, 1.0, None),
                 (sgu_width, F32, 1.0, None)], grp)
            kv_seq = lambda t: t.reshape(n_even, n_seq, seq_len, sb_width)
            if sample:
                o_a = _sb_sample(seq(q), kv_seq(k_all), kv_seq(v_all), cache_k, cache_v, i)
            else:
                o_a = _sb_prompt(seq(q), kv_seq(k_all), kv_seq(v_all), i, sb_heads)
            o_b, vn = _sgu(u, gv, g_sgu[i], sgu_mix, sgu_bias, i, emit_vn=sample)
            x = _outproj([o_a.reshape(x.shape[0], sb_width), o_b], w_out_even, i, x, mod, layer,
                         g_norm[layer, 1], grp)
            gvs.append(vn)
        else:
            q, k, v, r, a = _inproj(
                x, mod, layer, g_norm[layer, 0], w_in_odd, i,
                [(gla_qk, BF16, 1.0, None), (gla_qk, BF16, 1.0, None), (gla_vw, BF16, 1.0, None),
                 (gla_vw, BF16, 1.0, None)], grp, aux_w=w_a)
            o, states = _gla(seq(q), seq(k), seq(v), seq(r), seq(a), wg[0], wg[1], b_gate, g_gla,
                             state_gla if sample else None, i, gla_heads, states)
            x = _outproj([o.reshape(x.shape[0], gla_vw)], w_out_odd, i, x, mod, layer,
                         g_norm[layer, 1], grp)
        x = _ff(x, mod, layer, g_norm[layer, 2], g_norm[layer, 3], w_ff_up, w_ff_down, grp)
    return x.reshape(n_seq, seq_len, d), k_all, v_all, states, gvs


def kernel(x_prompt, x_sample, cache_sb_k, cache_sb_v, state_gla, c_prompt, c_sample, w_mod, b_mod, g_norm, w_in_even, w_out_even, g_sgu, w_sgu, b_sgu, w_in_odd, w_gate_up, b_gate, g_gla, w_out_odd, w_ff_up, w_ff_down):
    batch, seq, d = x_prompt.shape
    dec_batch, dec_seq, _ = x_sample.shape
    n_even, _, past, sb_heads, sb_dh = cache_sb_k.shape
    sb_width = sb_heads * sb_dh
    _, _, gla_heads, gla_dk, gla_dv = state_gla.shape
    gla_qk, gla_vw = gla_heads * gla_dk, gla_heads * gla_dv
    _, groups, sgu_chunk, _ = w_sgu.shape
    sgu_width = g_sgu.shape[1]
    rank = w_gate_up.shape[1]
    dims = (sb_heads, sb_width, sgu_width, gla_heads, gla_qk, gla_vw)

    mod = _modulation(jnp.concatenate([c_sample, c_prompt], axis=0), w_mod, b_mod)
    mod = mod.reshape(mod.shape[0], mod.shape[1], 1, mod.shape[2])

    cast = lambda t: t.astype(BF16)
    n_main = 2 * gla_qk + 2 * gla_vw
    w_a = jnp.pad(w_in_odd[:, :, n_main:], ((0, 0), (0, 0), (0, LANES - rank)))
    w_a_hi = cast(w_a)
    w_a = (w_a_hi, cast(w_a - w_a_hi.astype(F32)))
    wg = jnp.pad(w_gate_up, ((0, 0), (0, LANES - rank), (0, 0)))
    wg_hi = cast(wg)
    wg = (wg_hi, cast(wg - wg_hi.astype(F32)))
    w_tri = w_sgu * jnp.tril(jnp.ones((sgu_chunk, sgu_chunk), F32))

    def sgu_tables(chunk_len):
        mix = cast(w_tri[:, :, :chunk_len, :chunk_len])
        bias = jnp.repeat(jnp.swapaxes(b_sgu[:, :, :chunk_len], 1, 2), sgu_width // groups, axis=2)
        return mix, bias

    common = (g_norm, cast(w_in_even), cast(w_out_even), g_sgu)
    tail = (cast(w_in_odd), w_a, wg, b_gate, g_gla, cast(w_out_odd), cast(w_ff_up), cast(w_ff_down))

    grp_p = _Group(batch, seq, dec_batch, 512)
    grp_s = _Group(dec_batch, dec_seq, 0, 512)

    y_p, ks_p, vs_p, st_p, _ = _trunk(
        x_prompt, grp_p, mod, common + sgu_tables(sgu_chunk) + tail, dims, None, None, None)
    y_s, ks_s, vs_s, st_s, gv_s = _trunk(
        x_sample, grp_s, mod, common + sgu_tables(dec_seq) + tail, dims,
        cache_sb_k, cache_sb_v, state_gla)

    heads = lambda t, b, l: t.reshape(t.shape[0], b, l, sb_heads, sb_dh)
    return (y_p, y_s,
            heads(ks_p, batch, seq), heads(vs_p, batch, seq), st_p,
            heads(ks_s, dec_batch, dec_seq), heads(vs_s, dec_batch, dec_seq), st_s,
            jnp.stack(gv_s).reshape(len(gv_s), dec_batch, dec_seq, sgu_width))
```

```python
import functools

import jax
import jax.numpy as jnp
from jax import lax
from jax.experimental import pallas as pl
from jax.experimental.pallas import tpu as pltpu

F32 = jnp.float32
BF16 = jnp.bfloat16

EPS = 1e-6
GLA_CHUNK = 64
GLA_TAU = 16.0
SUBLANES = 8
GLA_DIRECT = 2
LANES = 128
MASKED_LOG = -1e30
VMEM_LIMIT_BYTES = 56 * 1024 * 1024


def _params(n_grid):
    return pltpu.CompilerParams(
        dimension_semantics=("arbitrary",) * n_grid,
        vmem_limit_bytes=VMEM_LIMIT_BYTES)


def _dot(a, b):
    return jnp.dot(a, b, preferred_element_type=F32)


def _dot_nt(a, b):
    return lax.dot_general(a, b, (((1,), (1,)), ((), ())), preferred_element_type=F32)


def _dot_tn(a, b):
    return lax.dot_general(a, b, (((0,), (0,)), ((), ())), preferred_element_type=F32)


def _split_bf16(x):
    hi = x.astype(BF16)
    lo = (x - hi.astype(F32)).astype(BF16)
    return hi, lo


def _rms(x, g):
    return x * lax.rsqrt(jnp.mean(x * x, axis=-1, keepdims=True) + EPS) * g


def _softplus(z):
    return jnp.maximum(z, 0.0) + jnp.log(1.0 + jnp.exp(-jnp.abs(z)))


def _silu(x):
    return x * jax.nn.sigmoid(x)


def _rows(ref, rows):
    v = ref[...]
    nb, _, d = v.shape
    if nb == 1:
        return v[0]
    return jnp.broadcast_to(v, (nb, rows, d)).reshape(nb * rows, d)


class _Group:
    def __init__(self, n_seq, seq_len, mod_row0, tm_pref):
        self.n_seq, self.seq_len, self.mod_row0 = n_seq, seq_len, mod_row0
        self.n = n_seq * seq_len
        if seq_len >= tm_pref:
            assert seq_len % tm_pref == 0
            self.tm, self.nb = tm_pref, 1
        else:
            self.nb = min(n_seq, tm_pref // seq_len)
            assert n_seq % self.nb == 0
            self.tm = self.nb * seq_len
        self.rows = self.tm // self.nb
        self.blocks_per_seq = max(1, seq_len // self.tm)

    def mod_spec(self, layer, col, d):
        nb, bps, row0 = self.nb, self.blocks_per_seq, self.mod_row0

        def index(i, *_):
            if nb == 1:
                return (layer, row0 + i // bps, 0, col)
            return (layer, row0 // nb + i, 0, col)

        return pl.BlockSpec((None, nb, 1, d), index)


def _mod_kernel(c_ref, w_ref, b_ref, o_ref):
    cs = _silu(c_ref[...]).astype(BF16)
    o_ref[...] = _dot(cs, w_ref[...].astype(BF16)) + b_ref[...]


def _modulation(c_all, w_mod, b_mod):
    depth, d, n_out = w_mod.shape
    n_c = c_all.shape[0]
    tn = min(n_out, 1024)
    return pl.pallas_call(
        _mod_kernel,
        out_shape=jax.ShapeDtypeStruct((depth, n_c, n_out), F32),
        grid=(depth, n_out // tn),
        in_specs=[
            pl.BlockSpec((n_c, d), lambda l, j: (0, 0)),
            pl.BlockSpec((None, d, tn), lambda l, j: (l, 0, j)),
            pl.BlockSpec((None, 1, tn), lambda l, j: (l, 0, j)),
        ],
        out_specs=pl.BlockSpec((None, n_c, tn), lambda l, j: (l, 0, j)),
        compiler_params=_params(2),
        name="modulation",
    )(c_all, w_mod, b_mod.reshape(depth, 1, n_out))


def _inproj_kernel(*refs, seg_cols, seg_scales, rows, has_aux, n_carried, n_slabs, layer_i):
    x_ref, g_ref, sh_ref, sc_ref, w_ref = refs[:5]
    pos = 5
    if has_aux:
        wah_ref, wal_ref = refs[5:7]
        pos = 7
    pos += n_carried
    out_refs = refs[pos:pos + len(seg_cols)]
    if has_aux:
        aux_ref = refs[pos + len(seg_cols)]
    tm = x_ref.shape[0]
    scale_rows, shift_rows = _rows(sc_ref, rows), _rows(sh_ref, rows)
    for s in range(n_slabs):
        rs = slice(s * tm // n_slabs, (s + 1) * tm // n_slabs)
        per_row = lambda v: v if v.shape[0] == 1 else v[rs]
        h = _rms(x_ref[rs, :], g_ref[...]) * (1.0 + per_row(scale_rows)) + per_row(shift_rows)
        h_hi, h_lo = _split_bf16(h)
        if has_aux:
            aux_ref[rs, :] = (_dot(h_hi, wah_ref[...]) + _dot(h_lo, wah_ref[...])
                              + _dot(h_hi, wal_ref[...]))
        for o_ref, (c0, c1), scale in zip(out_refs, seg_cols, seg_scales):
            y = _dot(h_hi, w_ref[:, c0:c1])
            if scale != 1.0:
                y = y * scale
            if len(o_ref.shape) == 2:
                o_ref[rs, :] = y.astype(o_ref.dtype)
            else:
                for l in range(o_ref.shape[0]):
                    o_ref[l, rs, :] = (y if l == layer_i else jnp.zeros_like(y)).astype(o_ref.dtype)


def _inproj(x, mod, layer, g, w, layer_i, segs, grp, aux_w=None):
    n, d = x.shape
    tm = grp.tm
    w_cols = w.shape[2]
    in_specs = [
        pl.BlockSpec((tm, d), lambda i: (i, 0)),
        pl.BlockSpec((1, d), lambda i: (0, 0)),
        grp.mod_spec(layer, 0, d),
        grp.mod_spec(layer, 1, d),
        pl.BlockSpec((None, d, w_cols), lambda i: (layer_i, 0, 0), pipeline_mode=pl.Buffered(1)),
    ]
    args = [x, g.reshape(1, d), mod, mod, w]
    has_aux = aux_w is not None
    if has_aux:
        wah, wal = aux_w
        wa = wah.shape[2]
        in_specs += [pl.BlockSpec((None, d, wa), lambda i: (layer_i, 0, 0))] * 2
        args += [wah, wal]
    out_shape, out_specs, aliases, seg_cols = [], [], {}, []
    c0 = 0
    for k, (wd, dt, _, stack) in enumerate(segs):
        seg_cols.append((c0, c0 + wd))
        c0 += wd
        if stack is None:
            out_shape.append(jax.ShapeDtypeStruct((n, wd), dt))
            out_specs.append(pl.BlockSpec((tm, wd), lambda i: (i, 0)))
        else:
            n_layers, previous = stack
            out_shape.append(jax.ShapeDtypeStruct((n_layers, n, wd), dt))
            if previous is None:
                out_specs.append(pl.BlockSpec((n_layers, tm, wd), lambda i: (0, i, 0)))
            else:
                out_specs.append(pl.BlockSpec((None, tm, wd), lambda i: (layer_i, i, 0)))
                aliases[len(args)] = k
                in_specs.append(pl.BlockSpec(memory_space=pl.ANY))
                args.append(previous)
    assert c0 <= w_cols and w.shape[1] == d
    if has_aux:
        out_shape.append(jax.ShapeDtypeStruct((n, wa), F32))
        out_specs.append(pl.BlockSpec((tm, wa), lambda i: (i, 0)))
    return pl.pallas_call(
        functools.partial(_inproj_kernel, seg_cols=tuple(seg_cols),
                          seg_scales=tuple(sg[2] for sg in segs), rows=grp.rows, has_aux=has_aux,
                          n_carried=len(aliases), n_slabs=2 if tm % 32 == 0 else 1,
                          layer_i=layer_i),
        out_shape=out_shape,
        grid=(n // tm,),
        in_specs=in_specs,
        out_specs=out_specs,
        input_output_aliases=aliases,
        compiler_params=_params(1),
        name="in_proj",
    )(*args)


def _outproj_kernel(*refs, n_a, rows):
    a_refs = refs[:n_a]
    w_refs = refs[n_a:2 * n_a]
    x_ref, g_ref, gt_ref, o_ref = refs[2 * n_a:]
    tm = x_ref.shape[0]
    n_slabs = 2 if tm % 32 == 0 else 1
    gate_rows = _rows(gt_ref, rows)
    for s in range(n_slabs):
        rs = slice(s * tm // n_slabs, (s + 1) * tm // n_slabs)
        y = _dot(a_refs[0][rs, :], w_refs[0][...])
        for a_ref, w_ref in zip(a_refs[1:], w_refs[1:]):
            y = y + _dot(a_ref[rs, :], w_ref[...])
        gate = gate_rows if gate_rows.shape[0] == 1 else gate_rows[rs]
        o_ref[rs, :] = x_ref[rs, :] + gate * _rms(y, g_ref[...])


def _outproj(acts, w, layer_i, x, mod, layer, g, grp):
    n, d = x.shape
    tm = grp.tm
    ka = acts[0].shape[1]
    assert all(a.shape[1] == ka for a in acts) and w.shape[1] == ka * len(acts)
    in_specs = [pl.BlockSpec((tm, ka), lambda i: (i, 0)) for _ in acts]
    in_specs += [pl.BlockSpec((None, ka, d), lambda i, r=r: (layer_i, r, 0)) for r in range(len(acts))]
    in_specs += [
        pl.BlockSpec((tm, d), lambda i: (i, 0)),
        pl.BlockSpec((1, d), lambda i: (0, 0)),
        grp.mod_spec(layer, 2, d),
    ]
    return pl.pallas_call(
        functools.partial(_outproj_kernel, n_a=len(acts), rows=grp.rows),
        out_shape=jax.ShapeDtypeStruct((n, d), F32),
        grid=(n // tm,),
        in_specs=in_specs,
        out_specs=pl.BlockSpec((tm, d), lambda i: (i, 0)),
        compiler_params=_params(1),
        name="out_proj",
    )(*acts, *([w] * len(acts)), x, g.reshape(1, d), mod)


def _ff_kernel(x_ref, g1_ref, sh_ref, sc_ref, wu_ref, wd_ref, g2_ref, gt_ref, o_ref,
               h_ref, t_ref, y_ref, *, rows, n_a, n_b):
    j = pl.program_id(1)
    tf = t_ref.shape[2]
    tn = y_ref.shape[2]

    def hidden(h):
        t = jnp.maximum(_dot(h, wu_ref[...]), 0.0)
        return (t * t).astype(BF16)

    @pl.when(j == 0)
    def _():
        tm = x_ref.shape[0]
        n_slabs = 2 if tm % 32 == 0 else 1
        scale_rows, shift_rows = _rows(sc_ref, rows), _rows(sh_ref, rows)
        for s in range(n_slabs):
            rs = slice(s * tm // n_slabs, (s + 1) * tm // n_slabs)
            per_row = lambda v: v if v.shape[0] == 1 else v[rs]
            h = _rms(x_ref[rs, :], g1_ref[...]) * (1.0 + per_row(scale_rows)) + per_row(shift_rows)
            h = h.astype(BF16)
            h_ref[rs, :] = h
            t_ref[0, rs, :] = hidden(h)

    @pl.when((j > 0) & (j < n_a))
    def _():
        t_ref[j] = hidden(h_ref[...])

    @pl.when(j >= n_a)
    def _():
        acc = _dot(t_ref[0], wd_ref[0:tf, :])
        for a in range(1, n_a):
            acc = acc + _dot(t_ref[a], wd_ref[a * tf:(a + 1) * tf, :])
        y_ref[j - n_a] = acc

    @pl.when(j == n_a + n_b - 1)
    def _():
        ss = jnp.sum(jnp.square(y_ref[0]), axis=-1, keepdims=True)
        for b in range(1, n_b):
            ss = ss + jnp.sum(jnp.square(y_ref[b]), axis=-1, keepdims=True)
        rs = lax.rsqrt(ss * (1.0 / (n_b * tn)) + EPS)
        gate = _rows(gt_ref, rows)
        g2 = g2_ref[...]
        for b in range(n_b):
            cs = slice(b * tn, (b + 1) * tn)
            o_ref[:, cs] = x_ref[:, cs] + gate[:, cs] * (y_ref[b] * rs * g2[:, cs])


def _ff(x, mod, layer, g1, g2, w_up, w_down, grp):
    n, d = x.shape
    f = w_up.shape[2]
    tm = grp.tm
    tf = min(1024, f)
    tn = min(4 * LANES, d)
    assert f % tf == 0 and d % tn == 0
    n_a, n_b = f // tf, d // tn
    return pl.pallas_call(
        functools.partial(_ff_kernel, rows=grp.rows, n_a=n_a, n_b=n_b),
        out_shape=jax.ShapeDtypeStruct((n, d), F32),
        grid=(n // tm, n_a + n_b),
        in_specs=[
            pl.BlockSpec((tm, d), lambda i, j: (i, 0)),
            pl.BlockSpec((1, d), lambda i, j: (0, 0)),
            grp.mod_spec(layer, 3, d),
            grp.mod_spec(layer, 4, d),
            pl.BlockSpec((None, d, tf), lambda i, j: (layer, 0, jnp.minimum(j, n_a - 1))),
            pl.BlockSpec((None, f, tn), lambda i, j: (layer, 0, jnp.clip(j - n_a, 0, n_b - 1))),
            pl.BlockSpec((1, d), lambda i, j: (0, 0)),
            grp.mod_spec(layer, 5, d),
        ],
        out_specs=pl.BlockSpec((tm, d), lambda i, j: (i, 0)),
        scratch_shapes=[pltpu.VMEM((tm, d), BF16), pltpu.VMEM((n_a, tm, tf), BF16),
                        pltpu.VMEM((n_b, tm, tn), F32)],
        compiler_params=_params(2),
        name="channel_mlp",
    )(x, g1.reshape(1, d), mod, mod, w_up, w_down, g2.reshape(1, d), mod)


def _strict_upper(n):
    r = lax.broadcasted_iota(jnp.int32, (n, n), 0)
    c = lax.broadcasted_iota(jnp.int32, (n, n), 1)
    return jnp.where(r > c, 1.0, 0.0).astype(BF16)


def _sb_prompt_kernel(q_ref, k_ref, v_ref, o_ref, kb_ref, vb_ref, lb_ref, hi_ref, lo_ref,
                      acc_ref, carry_ref, sum_ref, *, t, dh):
    qi = pl.program_id(2)
    heads = range(q_ref.shape[1] // dh)
    cols = [slice(h * dh, (h + 1) * dh) for h in heads]

    @pl.when(qi == 0)
    def _():
        kb_ref[...] = k_ref[...].astype(BF16)
        vb_ref[...] = v_ref[...].astype(BF16)

    u = _strict_upper(t)
    qs = [q_ref[:, cs] for cs in cols]

    def logits(start):
        return [_dot_nt(q, kb_ref[pl.ds(start, t), cs]) for q, cs in zip(qs, cols)]

    def stage1(zs, slot, diagonal):
        sps = [_softplus(z) for z in zs]
        if diagonal:
            r = lax.broadcasted_iota(jnp.int32, (t, t), 0)
            c = lax.broadcasted_iota(jnp.int32, (t, t), 1)
            negs = [jnp.where(c < r, -sp, 0.0) for sp in sps]
            log_betas = [jnp.where(c < r, z - sp, MASKED_LOG) for z, sp in zip(zs, sps)]
        else:
            negs = [-sp for sp in sps]
            log_betas = [z - sp for z, sp in zip(zs, sps)]
        for h in heads:
            lb_ref[slot, h] = log_betas[h]
            hi_ref[slot, h], lo_ref[slot, h] = _split_bf16(negs[h])
            sum_ref[h] = jnp.sum(negs[h], axis=-1, keepdims=True)

    def suffixes(slot):
        return [_dot(hi_ref[slot, h], u) + _dot(lo_ref[slot, h], u) for h in heads]

    def stage2(sufs, start, slot):
        ws = [jnp.exp(lb_ref[slot, h] + sufs[h] + carry_ref[h]).astype(BF16) for h in heads]
        for h in heads:
            acc_ref[h] += _dot(ws[h], vb_ref[pl.ds(start, t), cols[h]])

    acc_ref[...] = jnp.zeros_like(acc_ref)
    carry_ref[...] = jnp.zeros_like(carry_ref)
    stage1(logits(pl.multiple_of(qi * t, t)), 0, True)

    def body(it, _):
        slot = it % 2
        sufs = suffixes(slot)
        zs = logits(pl.multiple_of((qi - 1 - it) * t, t))
        stage2(sufs, pl.multiple_of((qi - it) * t, t), slot)
        for h in heads:
            carry_ref[h] += sum_ref[h]
        stage1(zs, 1 - slot, False)
        return 0

    lax.fori_loop(0, qi, body, 0)
    stage2(suffixes(qi % 2), 0, qi % 2)
    for h in heads:
        o_ref[:, cols[h]] = acc_ref[h].astype(o_ref.dtype)


def _sb_prompt(q, k, v, layer_i, n_heads, t_pref=256, heads_per_step=4):
    b, l, w = q.shape
    dh = w // n_heads
    t = min(t_pref, l)
    hs = min(heads_per_step, n_heads)
    assert l % t == 0 and n_heads % hs == 0
    q_spec = pl.BlockSpec((None, t, hs * dh), lambda bi, h, qi: (bi, qi, h))
    kv_spec = pl.BlockSpec((None, None, l, hs * dh), lambda bi, h, qi: (layer_i, bi, 0, h))
    return pl.pallas_call(
        functools.partial(_sb_prompt_kernel, t=t, dh=dh),
        out_shape=jax.ShapeDtypeStruct((b, l, w), BF16),
        grid=(b, n_heads // hs, l // t),
        in_specs=[q_spec, kv_spec, kv_spec],
        out_specs=q_spec,
        scratch_shapes=[pltpu.VMEM((l, hs * dh), BF16), pltpu.VMEM((l, hs * dh), BF16),
                        pltpu.VMEM((2, hs, t, t), F32), pltpu.VMEM((2, hs, t, t), BF16),
                        pltpu.VMEM((2, hs, t, t), BF16), pltpu.VMEM((hs, t, dh), F32),
                        pltpu.VMEM((hs, t, 1), F32), pltpu.VMEM((hs, t, 1), F32)],
        compiler_params=_params(3),
        name="stick_breaking_prompt",
    )(q, k, v)


def _sb_sample_kernel(q_ref, kn_ref, vn_ref, kc_ref, vc_ref, o_ref, *, tk, n_heads):
    lq, w = q_ref.shape
    dh = kc_ref.shape[1]
    past = kc_ref.shape[0] // n_heads
    nl = n_heads * lq

    qf = q_ref[...].astype(F32)
    row_h = lax.broadcasted_iota(jnp.int32, (nl, w), 0) // lq
    col_h = lax.broadcasted_iota(jnp.int32, (nl, w), 1) // dh
    q_exp = jnp.where(row_h == col_h, jnp.concatenate([qf] * n_heads, axis=0), 0.0).astype(BF16)

    def strict_lower_t(n):
        r = lax.broadcasted_iota(jnp.int32, (n, n), 0)
        c = lax.broadcasted_iota(jnp.int32, (n, n), 1)
        return jnp.where(c > r, 1.0, 0.0).astype(BF16)

    def block(k_heads, v_heads, carry, accs, new):
        n = k_heads[0].shape[0]
        z = _dot_nt(k_heads[0], q_exp[:, 0:dh])
        for h in range(1, n_heads):
            z = z + _dot_nt(k_heads[h], q_exp[:, h * dh:(h + 1) * dh])
        sp = _softplus(z)
        if new:
            key = lax.broadcasted_iota(jnp.int32, (n, nl), 0)
            qry = lax.broadcasted_iota(jnp.int32, (n, nl), 1) % lq
            mask = key < qry
            neg = jnp.where(mask, -sp, 0.0)
        else:
            neg = -sp
        hi, lo = _split_bf16(neg)
        ut = strict_lower_t(n)
        suffix = _dot(ut, hi) + _dot(ut, lo)
        wgt = jnp.exp(z - sp + suffix + carry)
        if new:
            wgt = jnp.where(mask, wgt, 0.0)
        wt = wgt.T.astype(BF16)
        accs = [acc + _dot(wt[h * lq:(h + 1) * lq, :], v_heads[h]) for h, acc in enumerate(accs)]
        carry = carry + jnp.sum(neg, axis=0, keepdims=True)
        return carry, accs

    pad = jnp.zeros((LANES - lq, w), F32)
    kn = jnp.concatenate([kn_ref[...], pad], axis=0).astype(BF16)
    vn = jnp.concatenate([vn_ref[...], pad], axis=0).astype(BF16)
    heads = lambda a: [a[:, h * dh:(h + 1) * dh] for h in range(n_heads)]
    carry, accs = block(heads(kn), heads(vn), jnp.zeros((1, nl), F32),
                        [jnp.zeros((lq, dh), F32)] * n_heads, True)
    for i in reversed(range(past // tk)):
        rows = [pl.ds(i * tk * n_heads + h, tk, stride=n_heads) for h in range(n_heads)]
        ks = [kc_ref[r, :].astype(BF16) for r in rows]
        vs = [vc_ref[r, :].astype(BF16) for r in rows]
        carry, accs = block(ks, vs, carry, accs, False)

    o_ref[...] = jnp.concatenate(accs, axis=1).astype(o_ref.dtype)


def _sb_sample(q, k_new, v_new, cache_k, cache_v, layer_i, tk_pref=256):
    b, lq, w = q.shape
    _, _, past, n_heads, dh = cache_k.shape
    tk = min(tk_pref, past)
    assert past % tk == 0 and lq <= LANES
    new_spec = pl.BlockSpec((None, lq, w), lambda bi: (bi, 0, 0))
    kv_new_spec = pl.BlockSpec((None, None, lq, w), lambda bi: (layer_i, bi, 0, 0))
    cache_spec = pl.BlockSpec((None, None, past * n_heads, dh), lambda bi: (layer_i, bi, 0, 0))
    flat = lambda c: c.reshape(c.shape[0], c.shape[1], past * n_heads, dh)
    return pl.pallas_call(
        functools.partial(_sb_sample_kernel, tk=tk, n_heads=n_heads),
        out_shape=jax.ShapeDtypeStruct((b, lq, w), BF16),
        grid=(b,),
        in_specs=[new_spec, kv_new_spec, kv_new_spec, cache_spec, cache_spec],
        out_specs=new_spec,
        compiler_params=_params(1),
        name="stick_breaking_sample",
    )(q, k_new, v_new, flat(cache_k), flat(cache_v))


def _sgu_kernel(u_ref, gv_ref, g_ref, w_ref, b_ref, o_ref, *rest, r, gd):
    vb_ref = rest[-1]
    rb, width = u_ref.shape
    vn = _rms(jax.nn.gelu(gv_ref[...]), g_ref[...])
    if len(rest) == 2:
        rest[0][...] = vn
    vb_ref[...] = vn.astype(BF16)

    def chunk(s, _):
        rs = pl.ds(pl.multiple_of(s * r, r), r)
        for gi in range(width // gd):
            cs = slice(gi * gd, (gi + 1) * gd)
            mix = _dot(w_ref[gi], vb_ref[rs, cs]) + b_ref[:, cs]
            o_ref[rs, cs] = (jax.nn.gelu(u_ref[rs, cs].astype(F32)) * mix).astype(o_ref.dtype)
        return 0

    lax.fori_loop(0, rb // r, chunk, 0)


def _sgu(u, gv, g_sgu, w_mix, bias, layer_i, emit_vn, rb_pref=512):
    n, width = u.shape
    _, groups, r, _ = w_mix.shape
    rb = max(r, min(rb_pref, n))
    assert n % rb == 0 and rb % r == 0
    row_spec = pl.BlockSpec((rb, width), lambda i: (i, 0))
    out_shape = [jax.ShapeDtypeStruct((n, width), BF16)]
    out_specs = [row_spec]
    if emit_vn:
        out_shape.append(jax.ShapeDtypeStruct((n, width), F32))
        out_specs.append(row_spec)
    res = pl.pallas_call(
        functools.partial(_sgu_kernel, r=r, gd=width // groups),
        out_shape=out_shape,
        grid=(n // rb,),
        in_specs=[
            row_spec, row_spec,
            pl.BlockSpec((1, width), lambda i: (0, 0)),
            pl.BlockSpec((None, groups, r, r), lambda i: (layer_i, 0, 0, 0)),
            pl.BlockSpec((None, r, width), lambda i: (layer_i, 0, 0)),
        ],
        out_specs=out_specs,
        scratch_shapes=[pltpu.VMEM((rb, width), BF16)],
        compiler_params=_params(1),
        name="spatial_gating",
    )(u, gv, g_sgu.reshape(1, width), w_mix, bias)
    return res if emit_vn else (res[0], None)


def _gla_kernel(*refs, lc, has_state, has_carried, q_scale):
    q_ref, k_ref, v_ref, r_ref, a_ref, wgh_ref, wgl_ref, bg_ref, gg_ref = refs[:9]
    s0_ref = refs[9] if has_state else None
    o_ref, s_ref, st_ref, lg_ref = refs[9 + has_state + has_carried:]
    tb = pl.program_id(2)
    n_rows, dk = q_ref.shape
    n_chunks = n_rows // lc

    @pl.when(tb == 0)
    def _():
        if has_state:
            st_ref[...] = s0_ref[...].T
        else:
            st_ref[...] = jnp.zeros_like(st_ref)

    a_hi, a_lo = _split_bf16(a_ref[...])
    pre = (_dot(a_hi, wgh_ref[...]) + _dot(a_lo, wgh_ref[...]) + _dot(a_hi, wgl_ref[...])
           + bg_ref[...])
    lg_ref[...] = -_softplus(-pre) * (1.0 / GLA_TAU)

    row = lax.broadcasted_iota(jnp.int32, (lc, lc), 0)
    col = lax.broadcasted_iota(jnp.int32, (lc, lc), 1)
    lower = jnp.where(col <= row, 1.0, 0.0).astype(BF16)

    chunks = [slice(c * lc, (c + 1) * lc) for c in range(n_chunks)]
    per_chunk = lambda x: [x[cs] for cs in chunks]

    def roll_in_group(x, d):
        x3 = x.reshape(n_rows // SUBLANES, SUBLANES, x.shape[1])
        return pltpu.roll(x3, d, 1).reshape(x.shape)

    lg_hi, lg_lo = _split_bf16(lg_ref[...])
    b = jnp.concatenate(
        [_dot(lower, hi) + _dot(lower, lo) for hi, lo in zip(per_chunk(lg_hi), per_chunk(lg_lo))],
        axis=0)
    q = q_ref[...].astype(F32) * q_scale
    k = k_ref[...].astype(F32)

    scores = [jnp.zeros((lc, lc), F32)] * n_chunks
    h = lc // 2
    while h >= GLA_DIRECT:
        b3 = b.reshape(n_rows // (2 * h), 2 * h, dk)
        f = jnp.exp(-jnp.abs(b3 - b3[:, h - 1:h, :])).reshape(n_rows, dk)
        qf, kf = (q * f).astype(BF16), (k * f).astype(BF16)
        mask = ((row // h) == (col // h) + 1) & ((col // h) % 2 == 0)
        scores = [sc + jnp.where(mask, _dot_nt(qc, kc), 0.0)
                  for sc, qc, kc in zip(scores, per_chunk(qf), per_chunk(kf))]
        h //= 2
    for d in range(GLA_DIRECT):
        if d == 0:
            pd = jnp.sum(q * k, axis=-1, keepdims=True)
        else:
            kd = roll_in_group(k, d)
            bd = roll_in_group(b, d)
            pd = jnp.sum(q * kd * jnp.exp(jnp.minimum(b - bd, 0.0)), axis=-1, keepdims=True)
        mask = (col == row - d) & ((row % GLA_DIRECT) >= d)
        scores = [sc + jnp.where(mask, pc, 0.0) for sc, pc in zip(scores, per_chunk(pd))]

    b_last = jnp.broadcast_to(b.reshape(n_chunks, lc, dk)[:, lc - 1:lc, :], (n_chunks, lc, dk))
    decay = [jnp.exp(bl[0:1, :]) for bl in per_chunk(b_last.reshape(n_rows, dk))]
    qd = per_chunk((q * jnp.exp(b)).astype(BF16))
    k_dec = per_chunk((k * jnp.exp(b_last.reshape(n_rows, dk) - b)).astype(BF16))
    vs = [v_ref[cs, :] for cs in chunks]
    intra = [_dot(sc.astype(BF16), vc) for sc, vc in zip(scores, vs)]
    kv = [_dot_tn(vc, kc) for vc, kc in zip(vs, k_dec)]

    st = st_ref[...]
    o_inter = _dot_nt(qd[0], st.astype(BF16))
    for c, cs in enumerate(chunks):
        o = o_inter + intra[c]
        st = st * decay[c] + kv[c]
        if c + 1 < n_chunks:
            o_inter = _dot_nt(qd[c + 1], st.astype(BF16))
        on = _rms(o, gg_ref[...])
        o_ref[cs, :] = (on * _silu(r_ref[cs, :].astype(F32))).astype(o_ref.dtype)
    st_ref[...] = st

    @pl.when(tb == pl.num_programs(2) - 1)
    def _():
        s_ref[...] = st_ref[...].T


def _gla(q, k, v, r, a, wg_hi, wg_lo, b_gate, g_gla, state, layer_i, n_heads, states_out,
         tb_pref=2048):
    b, l, wq = q.shape
    dk = wq // n_heads
    dv = v.shape[2] // n_heads
    ra = a.shape[2]
    lc = min(GLA_CHUNK, l)
    tb = min(tb_pref, l)
    assert l % tb == 0 and tb % lc == 0 and lc % (2 * SUBLANES) == 0
    qk_spec = pl.BlockSpec((None, tb, dk), lambda bi, h, t: (bi, t, h))
    vr_spec = pl.BlockSpec((None, tb, dv), lambda bi, h, t: (bi, t, h))
    in_specs = [
        qk_spec, qk_spec, vr_spec, vr_spec,
        pl.BlockSpec((None, tb, ra), lambda bi, h, t: (bi, t, 0)),
        pl.BlockSpec((None, ra, dk), lambda bi, h, t: (layer_i, 0, h)),
        pl.BlockSpec((None, ra, dk), lambda bi, h, t: (layer_i, 0, h)),
        pl.BlockSpec((None, 1, dk), lambda bi, h, t: (layer_i, 0, h)),
        pl.BlockSpec((None, 1, dv), lambda bi, h, t: (layer_i, 0, 0)),
    ]
    args = [q, k, v, r, a, wg_hi, wg_lo, b_gate.reshape(-1, 1, wq), g_gla.reshape(-1, 1, dv)]
    has_state = state is not None
    if has_state:
        in_specs.append(
            pl.BlockSpec((None, None, None, dk, dv), lambda bi, h, t: (layer_i, bi, h, 0, 0)))
        args.append(state)
    aliases = {}
    if states_out is not None:
        aliases[len(args)] = 1
        in_specs.append(pl.BlockSpec(memory_space=pl.ANY))
        args.append(states_out)
    n_layers = b_gate.shape[0]
    return pl.pallas_call(
        functools.partial(_gla_kernel, lc=lc, has_state=has_state,
                          has_carried=states_out is not None, q_scale=dk ** -0.5),
        out_shape=[jax.ShapeDtypeStruct((b, l, n_heads * dv), BF16),
                   jax.ShapeDtypeStruct((n_layers, b, n_heads, dk, dv), F32)],
        grid=(b, n_heads, l // tb),
        in_specs=in_specs,
        out_specs=[vr_spec, pl.BlockSpec((None, None, None, dk, dv),
                                         lambda bi, h, t: (layer_i, bi, h, 0, 0))],
        scratch_shapes=[pltpu.VMEM((dv, dk), F32), pltpu.VMEM((tb, dk), F32)],
        input_output_aliases=aliases,
        compiler_params=_params(3),
        name="gated_linear_attention",
    )(*args)


def _trunk(x3, grp, mod, weights, dims, cache_k, cache_v, state_gla):
    (g_norm, w_in_even, w_out_even, g_sgu, sgu_mix, sgu_bias, w_in_odd, w_a, wg, b_gate, g_gla,
     w_out_odd, w_ff_up, w_ff_down) = weights
    sb_heads, sb_width, sgu_width, gla_heads, gla_qk, gla_vw = dims
    n_seq, seq_len, d = x3.shape
    sample = cache_k is not None
    x = x3.reshape(n_seq * seq_len, d)
    seq = lambda t: t.reshape(n_seq, seq_len, t.shape[-1])
    depth = g_norm.shape[0]
    n_even = (depth + 1) // 2
    k_all = v_all = None
    states = jnp.zeros((depth // 2, n_seq, gla_heads, gla_qk // gla_heads, gla_vw // gla_heads), F32)
    gvs = []
    for layer in range(depth):
        i = layer // 2
        if layer % 2 == 0:
            sb_scale = (sb_width // sb_heads) ** -0.5
            q, k_all, v_all, u, gv = _inproj(
                x, mod, layer, g_norm[layer, 0], w_in_even, i,
                [(sb_width, BF16, sb_scale, None), (sb_width, F32, 1.0, (n_even, k_all)),
                 (sb_width, F32, 1.0, (n_even, v_all)), (sgu_width, BF16, 1.0, None),
                 (sgu_width, F32, 1.0, None)], grp)
            kv_seq = lambda t: t.reshape(n_even, n_seq, seq_len, sb_width)
            if sample:
                o_a = _sb_sample(seq(q), kv_seq(k_all), kv_seq(v_all), cache_k, cache_v, i)
            else:
                o_a = _sb_prompt(seq(q), kv_seq(k_all), kv_seq(v_all), i, sb_heads)
            o_b, vn = _sgu(u, gv, g_sgu[i], sgu_mix, sgu_bias, i, emit_vn=sample)
            x = _outproj([o_a.reshape(x.shape[0], sb_width), o_b], w_out_even, i, x, mod, layer,
                         g_norm[layer, 1], grp)
            gvs.append(vn)
        else:
            q, k, v, r, a = _inproj(
                x, mod, layer, g_norm[layer, 0], w_in_odd, i,
                [(gla_qk, BF16, 1.0, None), (gla_qk, BF16, 1.0, None), (gla_vw, BF16, 1.0, None),
                 (gla_vw, BF16, 1.0, None)], grp, aux_w=w_a)
            o, states = _gla(seq(q), seq(k), seq(v), seq(r), seq(a), wg[0], wg[1], b_gate, g_gla,
                             state_gla if sample else None, i, gla_heads, states)
            x = _outproj([o.reshape(x.shape[0], gla_vw)], w_out_odd, i, x, mod, layer,
                         g_norm[layer, 1], grp)
        x = _ff(x, mod, layer, g_norm[layer, 2], g_norm[layer, 3], w_ff_up, w_ff_down, grp)
    return x.reshape(n_seq, seq_len, d), k_all, v_all, states, gvs


def kernel(x_prompt, x_sample, cache_sb_k, cache_sb_v, state_gla, c_prompt, c_sample, w_mod, b_mod, g_norm, w_in_even, w_out_even, g_sgu, w_sgu, b_sgu, w_in_odd, w_gate_up, b_gate, g_gla, w_out_odd, w_ff_up, w_ff_down):
    batch, seq, d = x_prompt.shape
    dec_batch, dec_seq, _ = x_sample.shape
    n_even, _, past, sb_heads, sb_dh = cache_sb_k.shape
    sb_width = sb_heads * sb_dh
    _, _, gla_heads, gla_dk, gla_dv = state_gla.shape
    gla_qk, gla_vw = gla_heads * gla_dk, gla_heads * gla_dv
    _, groups, sgu_chunk, _ = w_sgu.shape
    sgu_width = g_sgu.shape[1]
    rank = w_gate_up.shape[1]
    dims = (sb_heads, sb_width, sgu_width, gla_heads, gla_qk, gla_vw)

    mod = _modulation(jnp.concatenate([c_sample, c_prompt], axis=0), w_mod, b_mod)
    mod = mod.reshape(mod.shape[0], mod.shape[1], 1, mod.shape[2])

    cast = lambda t: t.astype(BF16)
    n_main = 2 * gla_qk + 2 * gla_vw
    w_a = jnp.pad(w_in_odd[:, :, n_main:], ((0, 0), (0, 0), (0, LANES - rank)))
    w_a_hi = cast(w_a)
    w_a = (w_a_hi, cast(w_a - w_a_hi.astype(F32)))
    wg = jnp.pad(w_gate_up, ((0, 0), (0, LANES - rank), (0, 0)))
    wg_hi = cast(wg)
    wg = (wg_hi, cast(wg - wg_hi.astype(F32)))
    w_tri = w_sgu * jnp.tril(jnp.ones((sgu_chunk, sgu_chunk), F32))

    def sgu_tables(chunk_len):
        mix = cast(w_tri[:, :, :chunk_len, :chunk_len])
        bias = jnp.repeat(jnp.swapaxes(b_sgu[:, :, :chunk_len], 1, 2), sgu_width // groups, axis=2)
        return mix, bias

    common = (g_norm, cast(w_in_even), cast(w_out_even), g_sgu)
    tail = (cast(w_in_odd), w_a, wg, b_gate, g_gla, cast(w_out_odd), cast(w_ff_up), cast(w_ff_down))

    grp_p = _Group(batch, seq, dec_batch, 512)
    grp_s = _Group(dec_batch, dec_seq, 0, 512)

    y_p, ks_p, vs_p, st_p, _ = _trunk(
        x_prompt, grp_p, mod, common + sgu_tables(sgu_chunk) + tail, dims, None, None, None)
    y_s, ks_s, vs_s, st_s, gv_s = _trunk(
        x_sample, grp_s, mod, common + sgu_tables(dec_seq) + tail, dims,
        cache_sb_k, cache_sb_v, state_gla)

    heads = lambda t, b, l: t.reshape(t.shape[0], b, l, sb_heads, sb_dh)
    return (y_p, y_s,
            heads(ks_p, batch, seq), heads(vs_p, batch, seq), st_p,
            heads(ks_s, dec_batch, dec_seq), heads(vs_s, dec_batch, dec_seq), st_s,
            jnp.stack(gv_s).reshape(len(gv_s), dec_batch, dec_seq, sgu_width))
```

```python
import functools

import jax
import jax.numpy as jnp
from jax import lax
from jax.experimental import pallas as pl
from jax.experimental.pallas import tpu as pltpu

F32 = jnp.float32
BF16 = jnp.bfloat16

EPS = 1e-6
GLA_CHUNK = 64
GLA_TAU = 16.0
SUBLANES = 8
GLA_DIRECT = 2
LANES = 128
MASKED_LOG = -1e30
VMEM_LIMIT_BYTES = 56 * 1024 * 1024


def _params(n_grid):
    return pltpu.CompilerParams(
        dimension_semantics=("arbitrary",) * n_grid,
        vmem_limit_bytes=VMEM_LIMIT_BYTES)


def _dot(a, b):
    return jnp.dot(a, b, preferred_element_type=F32)


def _dot_nt(a, b):
    return lax.dot_general(a, b, (((1,), (1,)), ((), ())), preferred_element_type=F32)


def _dot_tn(a, b):
    return lax.dot_general(a, b, (((0,), (0,)), ((), ())), preferred_element_type=F32)


def _split_bf16(x):
    hi = x.astype(BF16)
    lo = (x - hi.astype(F32)).astype(BF16)
    return hi, lo


def _rms(x, g):
    return x * lax.rsqrt(jnp.mean(x * x, axis=-1, keepdims=True) + EPS) * g


def _softplus(z):
    return jnp.maximum(z, 0.0) + jnp.log(1.0 + jnp.exp(-jnp.abs(z)))


def _silu(x):
    return x * jax.nn.sigmoid(x)


def _rows(ref, rows):
    v = ref[...]
    nb, _, d = v.shape
    if nb == 1:
        return v[0]
    return jnp.broadcast_to(v, (nb, rows, d)).reshape(nb * rows, d)


class _Group:
    def __init__(self, n_seq, seq_len, mod_row0, tm_pref):
        self.n_seq, self.seq_len, self.mod_row0 = n_seq, seq_len, mod_row0
        self.n = n_seq * seq_len
        if seq_len >= tm_pref:
            assert seq_len % tm_pref == 0
            self.tm, self.nb = tm_pref, 1
        else:
            self.nb = min(n_seq, tm_pref // seq_len)
            assert n_seq % self.nb == 0
            self.tm = self.nb * seq_len
        self.rows = self.tm // self.nb
        self.blocks_per_seq = max(1, seq_len // self.tm)

    def mod_spec(self, layer, col, d):
        nb, bps, row0 = self.nb, self.blocks_per_seq, self.mod_row0

        def index(i, *_):
            if nb == 1:
                return (layer, row0 + i // bps, 0, col)
            return (layer, row0 // nb + i, 0, col)

        return pl.BlockSpec((None, nb, 1, d), index)


def _mod_kernel(c_ref, w_ref, b_ref, o_ref):
    cs = _silu(c_ref[...]).astype(BF16)
    o_ref[...] = _dot(cs, w_ref[...].astype(BF16)) + b_ref[...]


def _modulation(c_all, w_mod, b_mod):
    depth, d, n_out = w_mod.shape
    n_c = c_all.shape[0]
    tn = min(n_out, 1024)
    return pl.pallas_call(
        _mod_kernel,
        out_shape=jax.ShapeDtypeStruct((depth, n_c, n_out), F32),
        grid=(depth, n_out // tn),
        in_specs=[
            pl.BlockSpec((n_c, d), lambda l, j: (0, 0)),
            pl.BlockSpec((None, d, tn), lambda l, j: (l, 0, j)),
            pl.BlockSpec((None, 1, tn), lambda l, j: (l, 0, j)),
        ],
        out_specs=pl.BlockSpec((None, n_c, tn), lambda l, j: (l, 0, j)),
        compiler_params=_params(2),
        name="modulation",
    )(c_all, w_mod, b_mod.reshape(depth, 1, n_out))


def _inproj_kernel(*refs, seg_cols, seg_scales, rows, has_aux, n_carried, n_slabs, layer_i):
    x_ref, g_ref, sh_ref, sc_ref, w_ref = refs[:5]
    pos = 5
    if has_aux:
        wah_ref, wal_ref = refs[5:7]
        pos = 7
    pos += n_carried
    out_refs = refs[pos:pos + len(seg_cols)]
    if has_aux:
        aux_ref = refs[pos + len(seg_cols)]
    tm = x_ref.shape[0]
    scale_rows, shift_rows = _rows(sc_ref, rows), _rows(sh_ref, rows)
    for s in range(n_slabs):
        rs = slice(s * tm // n_slabs, (s + 1) * tm // n_slabs)
        per_row = lambda v: v if v.shape[0] == 1 else v[rs]
        h = _rms(x_ref[rs, :], g_ref[...]) * (1.0 + per_row(scale_rows)) + per_row(shift_rows)
        h_hi, h_lo = _split_bf16(h)
        if has_aux:
            aux_ref[rs, :] = (_dot(h_hi, wah_ref[...]) + _dot(h_lo, wah_ref[...])
                              + _dot(h_hi, wal_ref[...]))
        for o_ref, (c0, c1), scale in zip(out_refs, seg_cols, seg_scales):
            y = _dot(h_hi, w_ref[:, c0:c1])
            if scale != 1.0:
                y = y * scale
            if len(o_ref.shape) == 2:
                o_ref[rs, :] = y.astype(o_ref.dtype)
            else:
                for l in range(o_ref.shape[0]):
                    o_ref[l, rs, :] = (y if l == layer_i else jnp.zeros_like(y)).astype(o_ref.dtype)


def _inproj(x, mod, layer, g, w, layer_i, segs, grp, aux_w=None):
    n, d = x.shape
    tm = grp.tm
    w_cols = w.shape[2]
    in_specs = [
        pl.BlockSpec((tm, d), lambda i: (i, 0)),
        pl.BlockSpec((1, d), lambda i: (0, 0)),
        grp.mod_spec(layer, 0, d),
        grp.mod_spec(layer, 1, d),
        pl.BlockSpec((None, d, w_cols), lambda i: (layer_i, 0, 0), pipeline_mode=pl.Buffered(1)),
    ]
    args = [x, g.reshape(1, d), mod, mod, w]
    has_aux = aux_w is not None
    if has_aux:
        wah, wal = aux_w
        wa = wah.shape[2]
        in_specs += [pl.BlockSpec((None, d, wa), lambda i: (layer_i, 0, 0))] * 2
        args += [wah, wal]
    out_shape, out_specs, aliases, seg_cols = [], [], {}, []
    c0 = 0
    for k, (wd, dt, _, stack) in enumerate(segs):
        seg_cols.append((c0, c0 + wd))
        c0 += wd
        if stack is None:
            out_shape.append(jax.ShapeDtypeStruct((n, wd), dt))
            out_specs.append(pl.BlockSpec((tm, wd), lambda i: (i, 0)))
        else:
            n_layers, previous = stack
            out_shape.append(jax.ShapeDtypeStruct((n_layers, n, wd), dt))
            if previous is None:
                out_specs.append(pl.BlockSpec((n_layers, tm, wd), lambda i: (0, i, 0)))
            else:
                out_specs.append(pl.BlockSpec((None, tm, wd), lambda i: (layer_i, i, 0)))
                aliases[len(args)] = k
                in_specs.append(pl.BlockSpec(memory_space=pl.ANY))
                args.append(previous)
    assert c0 <= w_cols and w.shape[1] == d
    if has_aux:
        out_shape.append(jax.ShapeDtypeStruct((n, wa), F32))
        out_specs.append(pl.BlockSpec((tm, wa), lambda i: (i, 0)))
    return pl.pallas_call(
        functools.partial(_inproj_kernel, seg_cols=tuple(seg_cols),
                          seg_scales=tuple(sg[2] for sg in segs), rows=grp.rows, has_aux=has_aux,
                          n_carried=len(aliases), n_slabs=2 if tm % 32 == 0 else 1,
                          layer_i=layer_i),
        out_shape=out_shape,
        grid=(n // tm,),
        in_specs=in_specs,
        out_specs=out_specs,
        input_output_aliases=aliases,
        compiler_params=_params(1),
        name="in_proj",
    )(*args)


def _outproj_kernel(*refs, n_a, rows):
    a_refs = refs[:n_a]
    w_refs = refs[n_a:2 * n_a]
    x_ref, g_ref, gt_ref, o_ref = refs[2 * n_a:]
    tm = x_ref.shape[0]
    n_slabs = 2 if tm % 32 == 0 else 1
    gate_rows = _rows(gt_ref, rows)
    for s in range(n_slabs):
        rs = slice(s * tm // n_slabs, (s + 1) * tm // n_slabs)
        y = _dot(a_refs[0][rs, :], w_refs[0][...])
        for a_ref, w_ref in zip(a_refs[1:], w_refs[1:]):
            y = y + _dot(a_ref[rs, :], w_ref[...])
        gate = gate_rows if gate_rows.shape[0] == 1 else gate_rows[rs]
        o_ref[rs, :] = x_ref[rs, :] + gate * _rms(y, g_ref[...])


def _outproj(acts, w, layer_i, x, mod, layer, g, grp):
    n, d = x.shape
    tm = grp.tm
    ka = acts[0].shape[1]
    assert all(a.shape[1] == ka for a in acts) and w.shape[1] == ka * len(acts)
    in_specs = [pl.BlockSpec((tm, ka), lambda i: (i, 0)) for _ in acts]
    in_specs += [pl.BlockSpec((None, ka, d), lambda i, r=r: (layer_i, r, 0)) for r in range(len(acts))]
    in_specs += [
        pl.BlockSpec((tm, d), lambda i: (i, 0)),
        pl.BlockSpec((1, d), lambda i: (0, 0)),
        grp.mod_spec(layer, 2, d),
    ]
    return pl.pallas_call(
        functools.partial(_outproj_kernel, n_a=len(acts), rows=grp.rows),
        out_shape=jax.ShapeDtypeStruct((n, d), F32),
        grid=(n // tm,),
        in_specs=in_specs,
        out_specs=pl.BlockSpec((tm, d), lambda i: (i, 0)),
        compiler_params=_params(1),
        name="out_proj",
    )(*acts, *([w] * len(acts)), x, g.reshape(1, d), mod)


def _ff_kernel(x_ref, g1_ref, sh_ref, sc_ref, wu_ref, wd_ref, g2_ref, gt_ref, o_ref,
               h_ref, t_ref, y_ref, *, rows, n_a, n_b):
    j = pl.program_id(1)
    tf = t_ref.shape[2]
    tn = y_ref.shape[2]

    def hidden(h):
        t = jnp.maximum(_dot(h, wu_ref[...]), 0.0)
        return (t * t).astype(BF16)

    @pl.when(j == 0)
    def _():
        tm = x_ref.shape[0]
        n_slabs = 2 if tm % 32 == 0 else 1
        scale_rows, shift_rows = _rows(sc_ref, rows), _rows(sh_ref, rows)
        for s in range(n_slabs):
            rs = slice(s * tm // n_slabs, (s + 1) * tm // n_slabs)
            per_row = lambda v: v if v.shape[0] == 1 else v[rs]
            h = _rms(x_ref[rs, :], g1_ref[...]) * (1.0 + per_row(scale_rows)) + per_row(shift_rows)
            h = h.astype(BF16)
            h_ref[rs, :] = h
            t_ref[0, rs, :] = hidden(h)

    @pl.when((j > 0) & (j < n_a))
    def _():
        t_ref[j] = hidden(h_ref[...])

    @pl.when(j >= n_a)
    def _():
        acc = _dot(t_ref[0], wd_ref[0:tf, :])
        for a in range(1, n_a):
            acc = acc + _dot(t_ref[a], wd_ref[a * tf:(a + 1) * tf, :])
        y_ref[j - n_a] = acc

    @pl.when(j == n_a + n_b - 1)
    def _():
        ss = jnp.sum(jnp.square(y_ref[0]), axis=-1, keepdims=True)
        for b in range(1, n_b):
            ss = ss + jnp.sum(jnp.square(y_ref[b]), axis=-1, keepdims=True)
        rs = lax.rsqrt(ss * (1.0 / (n_b * tn)) + EPS)
        gate = _rows(gt_ref, rows)
        g2 = g2_ref[...]
        for b in range(n_b):
            cs = slice(b * tn, (b + 1) * tn)
            o_ref[:, cs] = x_ref[:, cs] + gate[:, cs] * (y_ref[b] * rs * g2[:, cs])


def _ff(x, mod, layer, g1, g2, w_up, w_down, grp):
    n, d = x.shape
    f = w_up.shape[2]
    tm = grp.tm
    tf = min(1024, f)
    tn = min(4 * LANES, d)
    assert f % tf == 0 and d % tn == 0
    n_a, n_b = f // tf, d // tn
    return pl.pallas_call(
        functools.partial(_ff_kernel, rows=grp.rows, n_a=n_a, n_b=n_b),
        out_shape=jax.ShapeDtypeStruct((n, d), F32),
        grid=(n // tm, n_a + n_b),
        in_specs=[
            pl.BlockSpec((tm, d), lambda i, j: (i, 0)),
            pl.BlockSpec((1, d), lambda i, j: (0, 0)),
            grp.mod_spec(layer, 3, d),
            grp.mod_spec(layer, 4, d),
            pl.BlockSpec((None, d, tf), lambda i, j: (layer, 0, jnp.minimum(j, n_a - 1))),
            pl.BlockSpec((None, f, tn), lambda i, j: (layer, 0, jnp.clip(j - n_a, 0, n_b - 1))),
            pl.BlockSpec((1, d), lambda i, j: (0, 0)),
            grp.mod_spec(layer, 5, d),
        ],
        out_specs=pl.BlockSpec((tm, d), lambda i, j: (i, 0)),
        scratch_shapes=[pltpu.VMEM((tm, d), BF16), pltpu.VMEM((n_a, tm, tf), BF16),
                        pltpu.VMEM((n_b, tm, tn), F32)],
        compiler_params=_params(2),
        name="channel_mlp",
    )(x, g1.reshape(1, d), mod, mod, w_up, w_down, g2.reshape(1, d), mod)


def _strict_upper(n):
    r = lax.broadcasted_iota(jnp.int32, (n, n), 0)
    c = lax.broadcasted_iota(jnp.int32, (n, n), 1)
    return jnp.where(r > c, 1.0, 0.0).astype(BF16)


def _sb_prompt_kernel(q_ref, k_ref, v_ref, o_ref, kb_ref, vb_ref, lb_ref, hi_ref, lo_ref,
                      acc_ref, carry_ref, sum_ref, *, t, dh):
    qi = pl.program_id(2)
    heads = range(q_ref.shape[1] // dh)
    cols = [slice(h * dh, (h + 1) * dh) for h in heads]

    @pl.when(qi == 0)
    def _():
        kb_ref[...] = k_ref[...].astype(BF16)
        vb_ref[...] = v_ref[...].astype(BF16)

    u = _strict_upper(t)
    qs = [q_ref[:, cs] for cs in cols]

    def logits(start):
        return [_dot_nt(q, kb_ref[pl.ds(start, t), cs]) for q, cs in zip(qs, cols)]

    def stage1(zs, slot, diagonal):
        sps = [_softplus(z) for z in zs]
        if diagonal:
            r = lax.broadcasted_iota(jnp.int32, (t, t), 0)
            c = lax.broadcasted_iota(jnp.int32, (t, t), 1)
            negs = [jnp.where(c < r, -sp, 0.0) for sp in sps]
            log_betas = [jnp.where(c < r, z - sp, MASKED_LOG) for z, sp in zip(zs, sps)]
        else:
            negs = [-sp for sp in sps]
            log_betas = [z - sp for z, sp in zip(zs, sps)]
        for h in heads:
            lb_ref[slot, h] = log_betas[h]
            hi_ref[slot, h], lo_ref[slot, h] = _split_bf16(negs[h])
            sum_ref[h] = jnp.sum(negs[h], axis=-1, keepdims=True)

    def suffixes(slot):
        return [_dot(hi_ref[slot, h], u) + _dot(lo_ref[slot, h], u) for h in heads]

    def stage2(sufs, start, slot):
        ws = [jnp.exp(lb_ref[slot, h] + sufs[h] + carry_ref[h]).astype(BF16) for h in heads]
        for h in heads:
            acc_ref[h] += _dot(ws[h], vb_ref[pl.ds(start, t), cols[h]])

    acc_ref[...] = jnp.zeros_like(acc_ref)
    carry_ref[...] = jnp.zeros_like(carry_ref)
    stage1(logits(pl.multiple_of(qi * t, t)), 0, True)

    def body(it, _):
        slot = it % 2
        sufs = suffixes(slot)
        zs = logits(pl.multiple_of((qi - 1 - it) * t, t))
        stage2(sufs, pl.multiple_of((qi - it) * t, t), slot)
        for h in heads:
            carry_ref[h] += sum_ref[h]
        stage1(zs, 1 - slot, False)
        return 0

    lax.fori_loop(0, qi, body, 0)
    stage2(suffixes(qi % 2), 0, qi % 2)
    for h in heads:
        o_ref[:, cols[h]] = acc_ref[h].astype(o_ref.dtype)


def _sb_prompt(q, k, v, layer_i, n_heads, t_pref=256, heads_per_step=4):
    b, l, w = q.shape
    dh = w // n_heads
    t = min(t_pref, l)
    hs = min(heads_per_step, n_heads)
    assert l % t == 0 and n_heads % hs == 0
    q_spec = pl.BlockSpec((None, t, hs * dh), lambda bi, h, qi: (bi, qi, h))
    kv_spec = pl.BlockSpec((None, None, l, hs * dh), lambda bi, h, qi: (layer_i, bi, 0, h))
    return pl.pallas_call(
        functools.partial(_sb_prompt_kernel, t=t, dh=dh),
        out_shape=jax.ShapeDtypeStruct((b, l, w), BF16),
        grid=(b, n_heads // hs, l // t),
        in_specs=[q_spec, kv_spec, kv_spec],
        out_specs=q_spec,
        scratch_shapes=[pltpu.VMEM((l, hs * dh), BF16), pltpu.VMEM((l, hs * dh), BF16),
                        pltpu.VMEM((2, hs, t, t), F32), pltpu.VMEM((2, hs, t, t), BF16),
                        pltpu.VMEM((2, hs, t, t), BF16), pltpu.VMEM((hs, t, dh), F32),
                        pltpu.VMEM((hs, t, 1), F32), pltpu.VMEM((hs, t, 1), F32)],
        compiler_params=_params(3),
        name="stick_breaking_prompt",
    )(q, k, v)


def _sb_sample_kernel(q_ref, kn_ref, vn_ref, kc_ref, vc_ref, o_ref, *, tk, n_heads):
    lq, w = q_ref.shape
    dh = kc_ref.shape[1]
    past = kc_ref.shape[0] // n_heads
    nl = n_heads * lq

    qf = q_ref[...].astype(F32)
    row_h = lax.broadcasted_iota(jnp.int32, (nl, w), 0) // lq
    col_h = lax.broadcasted_iota(jnp.int32, (nl, w), 1) // dh
    q_exp = jnp.where(row_h == col_h, jnp.concatenate([qf] * n_heads, axis=0), 0.0).astype(BF16)

    def strict_lower_t(n):
        r = lax.broadcasted_iota(jnp.int32, (n, n), 0)
        c = lax.broadcasted_iota(jnp.int32, (n, n), 1)
        return jnp.where(c > r, 1.0, 0.0).astype(BF16)

    def block(k_heads, v_heads, carry, accs, new):
        n = k_heads[0].shape[0]
        z = _dot_nt(k_heads[0], q_exp[:, 0:dh])
        for h in range(1, n_heads):
            z = z + _dot_nt(k_heads[h], q_exp[:, h * dh:(h + 1) * dh])
        sp = _softplus(z)
        if new:
            key = lax.broadcasted_iota(jnp.int32, (n, nl), 0)
            qry = lax.broadcasted_iota(jnp.int32, (n, nl), 1) % lq
            mask = key < qry
            neg = jnp.where(mask, -sp, 0.0)
        else:
            neg = -sp
        hi, lo = _split_bf16(neg)
        ut = strict_lower_t(n)
        suffix = _dot(ut, hi) + _dot(ut, lo)
        wgt = jnp.exp(z - sp + suffix + carry)
        if new:
            wgt = jnp.where(mask, wgt, 0.0)
        wt = wgt.T.astype(BF16)
        accs = [acc + _dot(wt[h * lq:(h + 1) * lq, :], v_heads[h]) for h, acc in enumerate(accs)]
        carry = carry + jnp.sum(neg, axis=0, keepdims=True)
        return carry, accs

    pad = jnp.zeros((LANES - lq, w), F32)
    kn = jnp.concatenate([kn_ref[...], pad], axis=0).astype(BF16)
    vn = jnp.concatenate([vn_ref[...], pad], axis=0).astype(BF16)
    heads = lambda a: [a[:, h * dh:(h + 1) * dh] for h in range(n_heads)]
    carry, accs = block(heads(kn), heads(vn), jnp.zeros((1, nl), F32),
                        [jnp.zeros((lq, dh), F32)] * n_heads, True)
    for i in reversed(range(past // tk)):
        rows = [pl.ds(i * tk * n_heads + h, tk, stride=n_heads) for h in range(n_heads)]
        ks = [kc_ref[r, :].astype(BF16) for r in rows]
        vs = [vc_ref[r, :].astype(BF16) for r in rows]
        carry, accs = block(ks, vs, carry, accs, False)

    o_ref[...] = jnp.concatenate(accs, axis=1).astype(o_ref.dtype)


def _sb_sample(q, k_new, v_new, cache_k, cache_v, layer_i, tk_pref=256):
    b, lq, w = q.shape
    _, _, past, n_heads, dh = cache_k.shape
    tk = min(tk_pref, past)
    assert past % tk == 0 and lq <= LANES
    new_spec = pl.BlockSpec((None, lq, w), lambda bi: (bi, 0, 0))
    kv_new_spec = pl.BlockSpec((None, None, lq, w), lambda bi: (layer_i, bi, 0, 0))
    cache_spec = pl.BlockSpec((None, None, past * n_heads, dh), lambda bi: (layer_i, bi, 0, 0))
    flat = lambda c: c.reshape(c.shape[0], c.shape[1], past * n_heads, dh)
    return pl.pallas_call(
        functools.partial(_sb_sample_kernel, tk=tk, n_heads=n_heads),
        out_shape=jax.ShapeDtypeStruct((b, lq, w), BF16),
        grid=(b,),
        in_specs=[new_spec, kv_new_spec, kv_new_spec, cache_spec, cache_spec],
        out_specs=new_spec,
        compiler_params=_params(1),
        name="stick_breaking_sample",
    )(q, k_new, v_new, flat(cache_k), flat(cache_v))


def _sgu_kernel(u_ref, gv_ref, g_ref, w_ref, b_ref, o_ref, *rest, r, gd):
    vb_ref = rest[-1]
    rb, width = u_ref.shape
    vn = _rms(jax.nn.gelu(gv_ref[...]), g_ref[...])
    if len(rest) == 2:
        rest[0][...] = vn
    vb_ref[...] = vn.astype(BF16)

    def chunk(s, _):
        rs = pl.ds(pl.multiple_of(s * r, r), r)
        for gi in range(width // gd):
            cs = slice(gi * gd, (gi + 1) * gd)
            mix = _dot(w_ref[gi], vb_ref[rs, cs]) + b_ref[:, cs]
            o_ref[rs, cs] = (jax.nn.gelu(u_ref[rs, cs].astype(F32)) * mix).astype(o_ref.dtype)
        return 0

    lax.fori_loop(0, rb // r, chunk, 0)


def _sgu(u, gv, g_sgu, w_mix, bias, layer_i, emit_vn, rb_pref=512):
    n, width = u.shape
    _, groups, r, _ = w_mix.shape
    rb = max(r, min(rb_pref, n))
    assert n % rb == 0 and rb % r == 0
    row_spec = pl.BlockSpec((rb, width), lambda i: (i, 0))
    out_shape = [jax.ShapeDtypeStruct((n, width), BF16)]
    out_specs = [row_spec]
    if emit_vn:
        out_shape.append(jax.ShapeDtypeStruct((n, width), F32))
        out_specs.append(row_spec)
    res = pl.pallas_call(
        functools.partial(_sgu_kernel, r=r, gd=width // groups),
        out_shape=out_shape,
        grid=(n // rb,),
        in_specs=[
            row_spec, row_spec,
            pl.BlockSpec((1, width), lambda i: (0, 0)),
            pl.BlockSpec((None, groups, r, r), lambda i: (layer_i, 0, 0, 0)),
            pl.BlockSpec((None, r, width), lambda i: (layer_i, 0, 0)),
        ],
        out_specs=out_specs,
        scratch_shapes=[pltpu.VMEM((rb, width), BF16)],
        compiler_params=_params(1),
        name="spatial_gating",
    )(u, gv, g_sgu.reshape(1, width), w_mix, bias)
    return res if emit_vn else (res[0], None)


def _gla_kernel(*refs, lc, has_state, has_carried, q_scale, layer_i):
    q_ref, k_ref, v_ref, r_ref, a_ref, wgh_ref, wgl_ref, bg_ref, gg_ref = refs[:9]
    s0_ref = refs[9] if has_state else None
    o_ref, s_ref, st_ref, lg_ref = refs[9 + has_state + has_carried:]
    tb = pl.program_id(2)
    n_rows, dk = q_ref.shape
    n_chunks = n_rows // lc

    @pl.when(tb == 0)
    def _():
        if has_state:
            st_ref[...] = s0_ref[...].T
        else:
            st_ref[...] = jnp.zeros_like(st_ref)

    a_hi, a_lo = _split_bf16(a_ref[...])
    pre = (_dot(a_hi, wgh_ref[...]) + _dot(a_lo, wgh_ref[...]) + _dot(a_hi, wgl_ref[...])
           + bg_ref[...])
    lg_ref[...] = -_softplus(-pre) * (1.0 / GLA_TAU)

    row = lax.broadcasted_iota(jnp.int32, (lc, lc), 0)
    col = lax.broadcasted_iota(jnp.int32, (lc, lc), 1)
    lower = jnp.where(col <= row, 1.0, 0.0).astype(BF16)

    chunks = [slice(c * lc, (c + 1) * lc) for c in range(n_chunks)]
    per_chunk = lambda x: [x[cs] for cs in chunks]

    def roll_in_group(x, d):
        x3 = x.reshape(n_rows // SUBLANES, SUBLANES, x.shape[1])
        return pltpu.roll(x3, d, 1).reshape(x.shape)

    lg_hi, lg_lo = _split_bf16(lg_ref[...])
    b = jnp.concatenate(
        [_dot(lower, hi) + _dot(lower, lo) for hi, lo in zip(per_chunk(lg_hi), per_chunk(lg_lo))],
        axis=0)
    q = q_ref[...].astype(F32) * q_scale
    k = k_ref[...].astype(F32)

    scores = [jnp.zeros((lc, lc), F32)] * n_chunks
    h = lc // 2
    while h >= GLA_DIRECT:
        b3 = b.reshape(n_rows // (2 * h), 2 * h, dk)
        f = jnp.exp(-jnp.abs(b3 - b3[:, h - 1:h, :])).reshape(n_rows, dk)
        qf, kf = (q * f).astype(BF16), (k * f).astype(BF16)
        mask = ((row // h) == (col // h) + 1) & ((col // h) % 2 == 0)
        scores = [sc + jnp.where(mask, _dot_nt(qc, kc), 0.0)
                  for sc, qc, kc in zip(scores, per_chunk(qf), per_chunk(kf))]
        h //= 2
    for d in range(GLA_DIRECT):
        if d == 0:
            pd = jnp.sum(q * k, axis=-1, keepdims=True)
        else:
            kd = roll_in_group(k, d)
            bd = roll_in_group(b, d)
            pd = jnp.sum(q * kd * jnp.exp(jnp.minimum(b - bd, 0.0)), axis=-1, keepdims=True)
        mask = (col == row - d) & ((row % GLA_DIRECT) >= d)
        scores = [sc + jnp.where(mask, pc, 0.0) for sc, pc in zip(scores, per_chunk(pd))]

    b_last = jnp.broadcast_to(b.reshape(n_chunks, lc, dk)[:, lc - 1:lc, :], (n_chunks, lc, dk))
    decay = [jnp.exp(bl[0:1, :]) for bl in per_chunk(b_last.reshape(n_rows, dk))]
    qd = per_chunk((q * jnp.exp(b)).astype(BF16))
    k_dec = per_chunk((k * jnp.exp(b_last.reshape(n_rows, dk) - b)).astype(BF16))
    vs = [v_ref[cs, :] for cs in chunks]
    intra = [_dot(sc.astype(BF16), vc) for sc, vc in zip(scores, vs)]
    kv = [_dot_tn(vc, kc) for vc, kc in zip(vs, k_dec)]

    st = st_ref[...]
    o_inter = _dot_nt(qd[0], st.astype(BF16))
    for c, cs in enumerate(chunks):
        o = o_inter + intra[c]
        st = st * decay[c] + kv[c]
        if c + 1 < n_chunks:
            o_inter = _dot_nt(qd[c + 1], st.astype(BF16))
        on = _rms(o, gg_ref[...])
        o_ref[cs, :] = (on * _silu(r_ref[cs, :].astype(F32))).astype(o_ref.dtype)
    st_ref[...] = st

    @pl.when(tb == pl.num_programs(2) - 1)
    def _():
        if has_carried:
            s_ref[...] = st_ref[...].T
        else:
            for l in range(s_ref.shape[0]):
                s_ref[l] = st_ref[...].T if l == layer_i else jnp.zeros(s_ref.shape[1:], F32)


def _gla(q, k, v, r, a, wg_hi, wg_lo, b_gate, g_gla, state, layer_i, n_heads, states_out,
         tb_pref=2048):
    b, l, wq = q.shape
    dk = wq // n_heads
    dv = v.shape[2] // n_heads
    ra = a.shape[2]
    lc = min(GLA_CHUNK, l)
    tb = min(tb_pref, l)
    assert l % tb == 0 and tb % lc == 0 and lc % (2 * SUBLANES) == 0
    qk_spec = pl.BlockSpec((None, tb, dk), lambda bi, h, t: (bi, t, h))
    vr_spec = pl.BlockSpec((None, tb, dv), lambda bi, h, t: (bi, t, h))
    in_specs = [
        qk_spec, qk_spec, vr_spec, vr_spec,
        pl.BlockSpec((None, tb, ra), lambda bi, h, t: (bi, t, 0)),
        pl.BlockSpec((None, ra, dk), lambda bi, h, t: (layer_i, 0, h)),
        pl.BlockSpec((None, ra, dk), lambda bi, h, t: (layer_i, 0, h)),
        pl.BlockSpec((None, 1, dk), lambda bi, h, t: (layer_i, 0, h)),
        pl.BlockSpec((None, 1, dv), lambda bi, h, t: (layer_i, 0, 0)),
    ]
    args = [q, k, v, r, a, wg_hi, wg_lo, b_gate.reshape(-1, 1, wq), g_gla.reshape(-1, 1, dv)]
    has_state = state is not None
    if has_state:
        in_specs.append(
            pl.BlockSpec((None, None, None, dk, dv), lambda bi, h, t: (layer_i, bi, h, 0, 0)))
        args.append(state)
    aliases = {}
    n_layers = b_gate.shape[0]
    if states_out is not None:
        aliases[len(args)] = 1
        in_specs.append(pl.BlockSpec(memory_space=pl.ANY))
        args.append(states_out)
        state_spec = pl.BlockSpec((None, None, None, dk, dv),
                                  lambda bi, h, t: (layer_i, bi, h, 0, 0))
    else:
        state_spec = pl.BlockSpec((n_layers, None, None, dk, dv), lambda bi, h, t: (0, bi, h, 0, 0))
    return pl.pallas_call(
        functools.partial(_gla_kernel, lc=lc, has_state=has_state,
                          has_carried=states_out is not None, q_scale=dk ** -0.5,
                          layer_i=layer_i),
        out_shape=[jax.ShapeDtypeStruct((b, l, n_heads * dv), BF16),
                   jax.ShapeDtypeStruct((n_layers, b, n_heads, dk, dv), F32)],
        grid=(b, n_heads, l // tb),
        in_specs=in_specs,
        out_specs=[vr_spec, state_spec],
        scratch_shapes=[pltpu.VMEM((dv, dk), F32), pltpu.VMEM((tb, dk), F32)],
        input_output_aliases=aliases,
        compiler_params=_params(3),
        name="gated_linear_attention",
    )(*args)


def _trunk(x3, grp, mod, weights, dims, cache_k, cache_v, state_gla):
    (g_norm, w_in_even, w_out_even, g_sgu, sgu_mix, sgu_bias, w_in_odd, w_a, wg, b_gate, g_gla,
     w_out_odd, w_ff_up, w_ff_down) = weights
    sb_heads, sb_width, sgu_width, gla_heads, gla_qk, gla_vw = dims
    n_seq, seq_len, d = x3.shape
    sample = cache_k is not None
    x = x3.reshape(n_seq * seq_len, d)
    seq = lambda t: t.reshape(n_seq, seq_len, t.shape[-1])
    depth = g_norm.shape[0]
    n_even = (depth + 1) // 2
    k_all = v_all = states = None
    gvs = []
    for layer in range(depth):
        i = layer // 2
        if layer % 2 == 0:
            sb_scale = (sb_width // sb_heads) ** -0.5
            q, k_all, v_all, u, gv = _inproj(
                x, mod, layer, g_norm[layer, 0], w_in_even, i,
                [(sb_width, BF16, sb_scale, None), (sb_width, F32, 1.0, (n_even, k_all)),
                 (sb_width, F32, 1.0, (n_even, v_all)), (sgu_width, BF16, 1.0, None),
                 (sgu_width, F32, 1.0, None)], grp)
            kv_seq = lambda t: t.reshape(n_even, n_seq, seq_len, sb_width)
            if sample:
                o_a = _sb_sample(seq(q), kv_seq(k_all), kv_seq(v_all), cache_k, cache_v, i)
            else:
                o_a = _sb_prompt(seq(q), kv_seq(k_all), kv_seq(v_all), i, sb_heads)
            o_b, vn = _sgu(u, gv, g_sgu[i], sgu_mix, sgu_bias, i, emit_vn=sample)
            x = _outproj([o_a.reshape(x.shape[0], sb_width), o_b], w_out_even, i, x, mod, layer,
                         g_norm[layer, 1], grp)
            gvs.append(vn)
        else:
            q, k, v, r, a = _inproj(
                x, mod, layer, g_norm[layer, 0], w_in_odd, i,
                [(gla_qk, BF16, 1.0, None), (gla_qk, BF16, 1.0, None), (gla_vw, BF16, 1.0, None),
                 (gla_vw, BF16, 1.0, None)], grp, aux_w=w_a)
            o, states = _gla(seq(q), seq(k), seq(v), seq(r), seq(a), wg[0], wg[1], b_gate, g_gla,
                             state_gla if sample else None, i, gla_heads, states)
            x = _outproj([o.reshape(x.shape[0], gla_vw)], w_out_odd, i, x, mod, layer,
                         g_norm[layer, 1], grp)
        x = _ff(x, mod, layer, g_norm[layer, 2], g_norm[layer, 3], w_ff_up, w_ff_down, grp)
    return x.reshape(n_seq, seq_len, d), k_all, v_all, states, gvs


def kernel(x_prompt, x_sample, cache_sb_k, cache_sb_v, state_gla, c_prompt, c_sample, w_mod, b_mod, g_norm, w_in_even, w_out_even, g_sgu, w_sgu, b_sgu, w_in_odd, w_gate_up, b_gate, g_gla, w_out_odd, w_ff_up, w_ff_down):
    batch, seq, d = x_prompt.shape
    dec_batch, dec_seq, _ = x_sample.shape
    n_even, _, past, sb_heads, sb_dh = cache_sb_k.shape
    sb_width = sb_heads * sb_dh
    _, _, gla_heads, gla_dk, gla_dv = state_gla.shape
    gla_qk, gla_vw = gla_heads * gla_dk, gla_heads * gla_dv
    _, groups, sgu_chunk, _ = w_sgu.shape
    sgu_width = g_sgu.shape[1]
    rank = w_gate_up.shape[1]
    dims = (sb_heads, sb_width, sgu_width, gla_heads, gla_qk, gla_vw)

    mod = _modulation(jnp.concatenate([c_sample, c_prompt], axis=0), w_mod, b_mod)
    mod = mod.reshape(mod.shape[0], mod.shape[1], 1, mod.shape[2])

    cast = lambda t: t.astype(BF16)
    n_main = 2 * gla_qk + 2 * gla_vw
    w_a = jnp.pad(w_in_odd[:, :, n_main:], ((0, 0), (0, 0), (0, LANES - rank)))
    w_a_hi = cast(w_a)
    w_a = (w_a_hi, cast(w_a - w_a_hi.astype(F32)))
    wg = jnp.pad(w_gate_up, ((0, 0), (0, LANES - rank), (0, 0)))
    wg_hi = cast(wg)
    wg = (wg_hi, cast(wg - wg_hi.astype(F32)))
    w_tri = w_sgu * jnp.tril(jnp.ones((sgu_chunk, sgu_chunk), F32))

    def sgu_tables(chunk_len):
        mix = cast(w_tri[:, :, :chunk_len, :chunk_len])
        bias = jnp.repeat(jnp.swapaxes(b_sgu[:, :, :chunk_len], 1, 2), sgu_width // groups, axis=2)
        return mix, bias

    common = (g_norm, cast(w_in_even), cast(w_out_even), g_sgu)
    tail = (cast(w_in_odd), w_a, wg, b_gate, g_gla, cast(w_out_odd), cast(w_ff_up), cast(w_ff_down))

    grp_p = _Group(batch, seq, dec_batch, 512)
    grp_s = _Group(dec_batch, dec_seq, 0, 512)

    y_p, ks_p, vs_p, st_p, _ = _trunk(
        x_prompt, grp_p, mod, common + sgu_tables(sgu_chunk) + tail, dims, None, None, None)
    y_s, ks_s, vs_s, st_s, gv_s = _trunk(
        x_sample, grp_s, mod, common + sgu_tables(dec_seq) + tail, dims,
        cache_sb_k, cache_sb_v, state_gla)

    heads = lambda t, b, l: t.reshape(t.shape[0], b, l, sb_heads, sb_dh)
    return (y_p, y_s,
            heads(ks_p, batch, seq), heads(vs_p, batch, seq), st_p,
            heads(ks_s, dec_batch, dec_seq), heads(vs_s, dec_batch, dec_seq), st_s,
            jnp.stack(gv_s).reshape(len(gv_s), dec_batch, dec_seq, sgu_width))
```

```python
import functools

import jax
import jax.numpy as jnp
from jax import lax
from jax.experimental import pallas as pl
from jax.experimental.pallas import tpu as pltpu

F32 = jnp.float32
BF16 = jnp.bfloat16

EPS = 1e-6
GLA_CHUNK = 64
GLA_TAU = 16.0
SUBLANES = 8
GLA_DIRECT = 2
LANES = 128
MASKED_LOG = -1e30
VMEM_LIMIT_BYTES = 60 * 1024 * 1024


def _params(n_grid):
    return pltpu.CompilerParams(
        dimension_semantics=("arbitrary",) * n_grid,
        vmem_limit_bytes=VMEM_LIMIT_BYTES)


def _dot(a, b):
    return jnp.dot(a, b, preferred_element_type=F32)


def _dot_nt(a, b):
    return lax.dot_general(a, b, (((1,), (1,)), ((), ())), preferred_element_type=F32)


def _dot_tn(a, b):
    return lax.dot_general(a, b, (((0,), (0,)), ((), ())), preferred_element_type=F32)


def _split_bf16(x):
    hi = x.astype(BF16)
    lo = (x - hi.astype(F32)).astype(BF16)
    return hi, lo


def _rms(x, g):
    return x * lax.rsqrt(jnp.mean(x * x, axis=-1, keepdims=True) + EPS) * g


def _softplus(z):
    return jnp.maximum(z, 0.0) + jnp.log(1.0 + jnp.exp(-jnp.abs(z)))


def _silu(x):
    return x * jax.nn.sigmoid(x)


def _rows(ref, rows):
    v = ref[...]
    nb, _, d = v.shape
    if nb == 1:
        return v[0]
    return jnp.broadcast_to(v, (nb, rows, d)).reshape(nb * rows, d)


class _Group:
    def __init__(self, n_seq, seq_len, mod_row0, tm_pref):
        self.n_seq, self.seq_len, self.mod_row0 = n_seq, seq_len, mod_row0
        self.n = n_seq * seq_len
        if seq_len >= tm_pref:
            assert seq_len % tm_pref == 0
            self.tm, self.nb = tm_pref, 1
        else:
            self.nb = min(n_seq, tm_pref // seq_len)
            assert n_seq % self.nb == 0
            self.tm = self.nb * seq_len
        self.rows = self.tm // self.nb
        self.blocks_per_seq = max(1, seq_len // self.tm)

    def mod_spec(self, layer, col, d):
        nb, bps, row0 = self.nb, self.blocks_per_seq, self.mod_row0

        def index(i, *_):
            if nb == 1:
                return (layer, row0 + i // bps, 0, col)
            return (layer, row0 // nb + i, 0, col)

        return pl.BlockSpec((None, nb, 1, d), index)


def _mod_kernel(c_ref, w_ref, b_ref, o_ref):
    cs = _silu(c_ref[...]).astype(BF16)
    o_ref[...] = _dot(cs, w_ref[...].astype(BF16)) + b_ref[...]


def _modulation(c_all, w_mod, b_mod):
    depth, d, n_out = w_mod.shape
    n_c = c_all.shape[0]
    tn = min(n_out, 1024)
    return pl.pallas_call(
        _mod_kernel,
        out_shape=jax.ShapeDtypeStruct((depth, n_c, n_out), F32),
        grid=(depth, n_out // tn),
        in_specs=[
            pl.BlockSpec((n_c, d), lambda l, j: (0, 0)),
            pl.BlockSpec((None, d, tn), lambda l, j: (l, 0, j)),
            pl.BlockSpec((None, 1, tn), lambda l, j: (l, 0, j)),
        ],
        out_specs=pl.BlockSpec((None, n_c, tn), lambda l, j: (l, 0, j)),
        compiler_params=_params(2),
        name="modulation",
    )(c_all, w_mod, b_mod.reshape(depth, 1, n_out))


def _inproj_kernel(*refs, seg_cols, seg_scales, rows, has_aux, n_carried, n_slabs, layer_i):
    x_ref, g_ref, sh_ref, sc_ref, w_ref = refs[:5]
    pos = 5
    if has_aux:
        wah_ref, wal_ref = refs[5:7]
        pos = 7
    pos += n_carried
    out_refs = refs[pos:pos + len(seg_cols)]
    if has_aux:
        aux_ref = refs[pos + len(seg_cols)]
    tm = x_ref.shape[0]
    scale_rows, shift_rows = _rows(sc_ref, rows), _rows(sh_ref, rows)
    for s in range(n_slabs):
        rs = slice(s * tm // n_slabs, (s + 1) * tm // n_slabs)
        per_row = lambda v: v if v.shape[0] == 1 else v[rs]
        h = _rms(x_ref[rs, :], g_ref[...]) * (1.0 + per_row(scale_rows)) + per_row(shift_rows)
        h_hi, h_lo = _split_bf16(h)
        if has_aux:
            aux_ref[rs, :] = (_dot(h_hi, wah_ref[...]) + _dot(h_lo, wah_ref[...])
                              + _dot(h_hi, wal_ref[...]))
        for o_ref, (c0, c1), scale in zip(out_refs, seg_cols, seg_scales):
            y = _dot(h_hi, w_ref[:, c0:c1])
            if scale != 1.0:
                y = y * scale
            if len(o_ref.shape) == 2:
                o_ref[rs, :] = y.astype(o_ref.dtype)
            else:
                for l in range(o_ref.shape[0]):
                    o_ref[l, rs, :] = (y if l == layer_i else jnp.zeros_like(y)).astype(o_ref.dtype)


def _inproj(x, mod, layer, g, w, layer_i, segs, grp, aux_w=None):
    n, d = x.shape
    tm = grp.tm
    w_cols = w.shape[2]
    in_specs = [
        pl.BlockSpec((tm, d), lambda i: (i, 0)),
        pl.BlockSpec((1, d), lambda i: (0, 0)),
        grp.mod_spec(layer, 0, d),
        grp.mod_spec(layer, 1, d),
        pl.BlockSpec((None, d, w_cols), lambda i: (layer_i, 0, 0), pipeline_mode=pl.Buffered(1)),
    ]
    args = [x, g.reshape(1, d), mod, mod, w]
    has_aux = aux_w is not None
    if has_aux:
        wah, wal = aux_w
        wa = wah.shape[2]
        in_specs += [pl.BlockSpec((None, d, wa), lambda i: (layer_i, 0, 0))] * 2
        args += [wah, wal]
    out_shape, out_specs, aliases, seg_cols = [], [], {}, []
    c0 = 0
    for k, (wd, dt, _, stack) in enumerate(segs):
        seg_cols.append((c0, c0 + wd))
        c0 += wd
        if stack is None:
            out_shape.append(jax.ShapeDtypeStruct((n, wd), dt))
            out_specs.append(pl.BlockSpec((tm, wd), lambda i: (i, 0)))
        else:
            n_layers, previous = stack
            out_shape.append(jax.ShapeDtypeStruct((n_layers, n, wd), dt))
            if previous is None:
                out_specs.append(pl.BlockSpec((n_layers, tm, wd), lambda i: (0, i, 0)))
            else:
                out_specs.append(pl.BlockSpec((None, tm, wd), lambda i: (layer_i, i, 0)))
                aliases[len(args)] = k
                in_specs.append(pl.BlockSpec(memory_space=pl.ANY))
                args.append(previous)
    assert c0 <= w_cols and w.shape[1] == d
    if has_aux:
        out_shape.append(jax.ShapeDtypeStruct((n, wa), F32))
        out_specs.append(pl.BlockSpec((tm, wa), lambda i: (i, 0)))
    return pl.pallas_call(
        functools.partial(_inproj_kernel, seg_cols=tuple(seg_cols),
                          seg_scales=tuple(sg[2] for sg in segs), rows=grp.rows, has_aux=has_aux,
                          n_carried=len(aliases), n_slabs=2 if tm % 32 == 0 else 1,
                          layer_i=layer_i),
        out_shape=out_shape,
        grid=(n // tm,),
        in_specs=in_specs,
        out_specs=out_specs,
        input_output_aliases=aliases,
        compiler_params=_params(1),
        name="in_proj",
    )(*args)


def _outproj_kernel(*refs, n_a, rows):
    a_refs = refs[:n_a]
    w_refs = refs[n_a:2 * n_a]
    x_ref, g_ref, gt_ref, o_ref = refs[2 * n_a:]
    tm = x_ref.shape[0]
    n_slabs = 2 if tm % 32 == 0 else 1
    gate_rows = _rows(gt_ref, rows)
    for s in range(n_slabs):
        rs = slice(s * tm // n_slabs, (s + 1) * tm // n_slabs)
        y = _dot(a_refs[0][rs, :], w_refs[0][...])
        for a_ref, w_ref in zip(a_refs[1:], w_refs[1:]):
            y = y + _dot(a_ref[rs, :], w_ref[...])
        gate = gate_rows if gate_rows.shape[0] == 1 else gate_rows[rs]
        o_ref[rs, :] = x_ref[rs, :] + gate * _rms(y, g_ref[...])


def _outproj(acts, w, layer_i, x, mod, layer, g, grp):
    n, d = x.shape
    tm = grp.tm
    ka = acts[0].shape[1]
    assert all(a.shape[1] == ka for a in acts) and w.shape[1] == ka * len(acts)
    in_specs = [pl.BlockSpec((tm, ka), lambda i: (i, 0)) for _ in acts]
    in_specs += [pl.BlockSpec((None, ka, d), lambda i, r=r: (layer_i, r, 0)) for r in range(len(acts))]
    in_specs += [
        pl.BlockSpec((tm, d), lambda i: (i, 0)),
        pl.BlockSpec((1, d), lambda i: (0, 0)),
        grp.mod_spec(layer, 2, d),
    ]
    return pl.pallas_call(
        functools.partial(_outproj_kernel, n_a=len(acts), rows=grp.rows),
        out_shape=jax.ShapeDtypeStruct((n, d), F32),
        grid=(n // tm,),
        in_specs=in_specs,
        out_specs=pl.BlockSpec((tm, d), lambda i: (i, 0)),
        compiler_params=_params(1),
        name="out_proj",
    )(*acts, *([w] * len(acts)), x, g.reshape(1, d), mod)


def _ff_kernel(x_ref, g1_ref, sh_ref, sc_ref, wu_ref, wd_ref, g2_ref, gt_ref, o_ref,
               h_ref, t_ref, y_ref, *, rows, n_a, n_b):
    j = pl.program_id(1)
    tf = t_ref.shape[2]
    tn = y_ref.shape[2]

    def hidden(h):
        t = jnp.maximum(_dot(h, wu_ref[...]), 0.0)
        return (t * t).astype(BF16)

    @pl.when(j == 0)
    def _():
        tm = x_ref.shape[0]
        n_slabs = 2 if tm % 32 == 0 else 1
        scale_rows, shift_rows = _rows(sc_ref, rows), _rows(sh_ref, rows)
        for s in range(n_slabs):
            rs = slice(s * tm // n_slabs, (s + 1) * tm // n_slabs)
            per_row = lambda v: v if v.shape[0] == 1 else v[rs]
            h = _rms(x_ref[rs, :], g1_ref[...]) * (1.0 + per_row(scale_rows)) + per_row(shift_rows)
            h = h.astype(BF16)
            h_ref[rs, :] = h
            t_ref[0, rs, :] = hidden(h)

    @pl.when((j > 0) & (j < n_a))
    def _():
        t_ref[j] = hidden(h_ref[...])

    @pl.when(j >= n_a)
    def _():
        acc = _dot(t_ref[0], wd_ref[0:tf, :])
        for a in range(1, n_a):
            acc = acc + _dot(t_ref[a], wd_ref[a * tf:(a + 1) * tf, :])
        y_ref[j - n_a] = acc

    @pl.when(j == n_a + n_b - 1)
    def _():
        ss = jnp.sum(jnp.square(y_ref[0]), axis=-1, keepdims=True)
        for b in range(1, n_b):
            ss = ss + jnp.sum(jnp.square(y_ref[b]), axis=-1, keepdims=True)
        rs = lax.rsqrt(ss * (1.0 / (n_b * tn)) + EPS)
        gate = _rows(gt_ref, rows)
        g2 = g2_ref[...]
        for b in range(n_b):
            cs = slice(b * tn, (b + 1) * tn)
            o_ref[:, cs] = x_ref[:, cs] + gate[:, cs] * (y_ref[b] * rs * g2[:, cs])


def _ff(x, mod, layer, g1, g2, w_up, w_down, grp):
    n, d = x.shape
    f = w_up.shape[2]
    tm = grp.tm
    tf = min(2048, f)
    tn = min(4 * LANES, d)
    assert f % tf == 0 and d % tn == 0
    n_a, n_b = f // tf, d // tn
    return pl.pallas_call(
        functools.partial(_ff_kernel, rows=grp.rows, n_a=n_a, n_b=n_b),
        out_shape=jax.ShapeDtypeStruct((n, d), F32),
        grid=(n // tm, n_a + n_b),
        in_specs=[
            pl.BlockSpec((tm, d), lambda i, j: (i, 0), pipeline_mode=pl.Buffered(1)),
            pl.BlockSpec((1, d), lambda i, j: (0, 0)),
            grp.mod_spec(layer, 3, d),
            grp.mod_spec(layer, 4, d),
            pl.BlockSpec((None, d, tf), lambda i, j: (layer, 0, jnp.minimum(j, n_a - 1))),
            pl.BlockSpec((None, f, tn), lambda i, j: (layer, 0, jnp.clip(j - n_a, 0, n_b - 1))),
            pl.BlockSpec((1, d), lambda i, j: (0, 0)),
            grp.mod_spec(layer, 5, d),
        ],
        out_specs=pl.BlockSpec((tm, d), lambda i, j: (i, 0)),
        scratch_shapes=[pltpu.VMEM((tm, d), BF16), pltpu.VMEM((n_a, tm, tf), BF16),
                        pltpu.VMEM((n_b, tm, tn), F32)],
        compiler_params=_params(2),
        name="channel_mlp",
    )(x, g1.reshape(1, d), mod, mod, w_up, w_down, g2.reshape(1, d), mod)


def _strict_upper(n):
    r = lax.broadcasted_iota(jnp.int32, (n, n), 0)
    c = lax.broadcasted_iota(jnp.int32, (n, n), 1)
    return jnp.where(r > c, 1.0, 0.0).astype(BF16)


def _sb_prompt_kernel(q_ref, k_ref, v_ref, o_ref, kb_ref, vb_ref, lb_ref, hi_ref, lo_ref,
                      acc_ref, carry_ref, sum_ref, *, t, dh):
    qi = pl.program_id(2)
    heads = range(q_ref.shape[1] // dh)
    cols = [slice(h * dh, (h + 1) * dh) for h in heads]

    @pl.when(qi == 0)
    def _():
        kb_ref[...] = k_ref[...].astype(BF16)
        vb_ref[...] = v_ref[...].astype(BF16)

    u = _strict_upper(t)
    qs = [q_ref[:, cs] for cs in cols]

    def logits(start):
        return [_dot_nt(q, kb_ref[pl.ds(start, t), cs]) for q, cs in zip(qs, cols)]

    def stage1(zs, slot, diagonal):
        sps = [_softplus(z) for z in zs]
        if diagonal:
            r = lax.broadcasted_iota(jnp.int32, (t, t), 0)
            c = lax.broadcasted_iota(jnp.int32, (t, t), 1)
            negs = [jnp.where(c < r, -sp, 0.0) for sp in sps]
            log_betas = [jnp.where(c < r, z - sp, MASKED_LOG) for z, sp in zip(zs, sps)]
        else:
            negs = [-sp for sp in sps]
            log_betas = [z - sp for z, sp in zip(zs, sps)]
        for h in heads:
            lb_ref[slot, h] = log_betas[h]
            hi_ref[slot, h], lo_ref[slot, h] = _split_bf16(negs[h])
            sum_ref[h] = jnp.sum(negs[h], axis=-1, keepdims=True)

    def suffixes(slot):
        return [_dot(hi_ref[slot, h], u) + _dot(lo_ref[slot, h], u) for h in heads]

    def stage2(sufs, start, slot):
        ws = [jnp.exp(lb_ref[slot, h] + sufs[h] + carry_ref[h]).astype(BF16) for h in heads]
        for h in heads:
            acc_ref[h] += _dot(ws[h], vb_ref[pl.ds(start, t), cols[h]])

    acc_ref[...] = jnp.zeros_like(acc_ref)
    carry_ref[...] = jnp.zeros_like(carry_ref)
    stage1(logits(pl.multiple_of(qi * t, t)), 0, True)

    def body(it, _):
        slot = it % 2
        sufs = suffixes(slot)
        zs = logits(pl.multiple_of((qi - 1 - it) * t, t))
        stage2(sufs, pl.multiple_of((qi - it) * t, t), slot)
        for h in heads:
            carry_ref[h] += sum_ref[h]
        stage1(zs, 1 - slot, False)
        return 0

    lax.fori_loop(0, qi, body, 0)
    stage2(suffixes(qi % 2), 0, qi % 2)
    for h in heads:
        o_ref[:, cols[h]] = acc_ref[h].astype(o_ref.dtype)


def _sb_prompt(q, k, v, layer_i, n_heads, t_pref=256, heads_per_step=4):
    b, l, w = q.shape
    dh = w // n_heads
    t = min(t_pref, l)
    hs = min(heads_per_step, n_heads)
    assert l % t == 0 and n_heads % hs == 0
    q_spec = pl.BlockSpec((None, t, hs * dh), lambda bi, h, qi: (bi, qi, h))
    kv_spec = pl.BlockSpec((None, None, l, hs * dh), lambda bi, h, qi: (layer_i, bi, 0, h))
    return pl.pallas_call(
        functools.partial(_sb_prompt_kernel, t=t, dh=dh),
        out_shape=jax.ShapeDtypeStruct((b, l, w), BF16),
        grid=(b, n_heads // hs, l // t),
        in_specs=[q_spec, kv_spec, kv_spec],
        out_specs=q_spec,
        scratch_shapes=[pltpu.VMEM((l, hs * dh), BF16), pltpu.VMEM((l, hs * dh), BF16),
                        pltpu.VMEM((2, hs, t, t), F32), pltpu.VMEM((2, hs, t, t), BF16),
                        pltpu.VMEM((2, hs, t, t), BF16), pltpu.VMEM((hs, t, dh), F32),
                        pltpu.VMEM((hs, t, 1), F32), pltpu.VMEM((hs, t, 1), F32)],
        compiler_params=_params(3),
        name="stick_breaking_prompt",
    )(q, k, v)


def _sb_sample_kernel(q_ref, kn_ref, vn_ref, kc_ref, vc_ref, o_ref, *, tk, n_heads):
    lq, w = q_ref.shape
    dh = kc_ref.shape[1]
    past = kc_ref.shape[0] // n_heads
    nl = n_heads * lq

    qf = q_ref[...].astype(F32)
    row_h = lax.broadcasted_iota(jnp.int32, (nl, w), 0) // lq
    col_h = lax.broadcasted_iota(jnp.int32, (nl, w), 1) // dh
    q_exp = jnp.where(row_h == col_h, jnp.concatenate([qf] * n_heads, axis=0), 0.0).astype(BF16)

    def strict_lower_t(n):
        r = lax.broadcasted_iota(jnp.int32, (n, n), 0)
        c = lax.broadcasted_iota(jnp.int32, (n, n), 1)
        return jnp.where(c > r, 1.0, 0.0).astype(BF16)

    def block(k_heads, v_heads, carry, accs, new):
        n = k_heads[0].shape[0]
        z = _dot_nt(k_heads[0], q_exp[:, 0:dh])
        for h in range(1, n_heads):
            z = z + _dot_nt(k_heads[h], q_exp[:, h * dh:(h + 1) * dh])
        sp = _softplus(z)
        if new:
            key = lax.broadcasted_iota(jnp.int32, (n, nl), 0)
            qry = lax.broadcasted_iota(jnp.int32, (n, nl), 1) % lq
            mask = key < qry
            neg = jnp.where(mask, -sp, 0.0)
        else:
            neg = -sp
        hi, lo = _split_bf16(neg)
        ut = strict_lower_t(n)
        suffix = _dot(ut, hi) + _dot(ut, lo)
        wgt = jnp.exp(z - sp + suffix + carry)
        if new:
            wgt = jnp.where(mask, wgt, 0.0)
        wt = wgt.T.astype(BF16)
        accs = [acc + _dot(wt[h * lq:(h + 1) * lq, :], v_heads[h]) for h, acc in enumerate(accs)]
        carry = carry + jnp.sum(neg, axis=0, keepdims=True)
        return carry, accs

    pad = jnp.zeros((LANES - lq, w), F32)
    kn = jnp.concatenate([kn_ref[...], pad], axis=0).astype(BF16)
    vn = jnp.concatenate([vn_ref[...], pad], axis=0).astype(BF16)
    heads = lambda a: [a[:, h * dh:(h + 1) * dh] for h in range(n_heads)]
    carry, accs = block(heads(kn), heads(vn), jnp.zeros((1, nl), F32),
                        [jnp.zeros((lq, dh), F32)] * n_heads, True)
    for i in reversed(range(past // tk)):
        rows = [pl.ds(i * tk * n_heads + h, tk, stride=n_heads) for h in range(n_heads)]
        ks = [kc_ref[r, :].astype(BF16) for r in rows]
        vs = [vc_ref[r, :].astype(BF16) for r in rows]
        carry, accs = block(ks, vs, carry, accs, False)

    o_ref[...] = jnp.concatenate(accs, axis=1).astype(o_ref.dtype)


def _sb_sample(q, k_new, v_new, cache_k, cache_v, layer_i, tk_pref=256):
    b, lq, w = q.shape
    _, _, past, n_heads, dh = cache_k.shape
    tk = min(tk_pref, past)
    assert past % tk == 0 and lq <= LANES
    new_spec = pl.BlockSpec((None, lq, w), lambda bi: (bi, 0, 0))
    kv_new_spec = pl.BlockSpec((None, None, lq, w), lambda bi: (layer_i, bi, 0, 0))
    cache_spec = pl.BlockSpec((None, None, past * n_heads, dh), lambda bi: (layer_i, bi, 0, 0))
    flat = lambda c: c.reshape(c.shape[0], c.shape[1], past * n_heads, dh)
    return pl.pallas_call(
        functools.partial(_sb_sample_kernel, tk=tk, n_heads=n_heads),
        out_shape=jax.ShapeDtypeStruct((b, lq, w), BF16),
        grid=(b,),
        in_specs=[new_spec, kv_new_spec, kv_new_spec, cache_spec, cache_spec],
        out_specs=new_spec,
        compiler_params=_params(1),
        name="stick_breaking_sample",
    )(q, k_new, v_new, flat(cache_k), flat(cache_v))


def _sgu_kernel(u_ref, gv_ref, g_ref, w_ref, b_ref, o_ref, *rest, r, gd):
    vb_ref = rest[-1]
    rb, width = u_ref.shape
    vn = _rms(jax.nn.gelu(gv_ref[...]), g_ref[...])
    if len(rest) == 2:
        rest[0][...] = vn
    vb_ref[...] = vn.astype(BF16)

    def chunk(s, _):
        rs = pl.ds(pl.multiple_of(s * r, r), r)
        for gi in range(width // gd):
            cs = slice(gi * gd, (gi + 1) * gd)
            mix = _dot(w_ref[gi], vb_ref[rs, cs]) + b_ref[:, cs]
            o_ref[rs, cs] = (jax.nn.gelu(u_ref[rs, cs].astype(F32)) * mix).astype(o_ref.dtype)
        return 0

    lax.fori_loop(0, rb // r, chunk, 0)


def _sgu(u, gv, g_sgu, w_mix, bias, layer_i, emit_vn, rb_pref=512):
    n, width = u.shape
    _, groups, r, _ = w_mix.shape
    rb = max(r, min(rb_pref, n))
    assert n % rb == 0 and rb % r == 0
    row_spec = pl.BlockSpec((rb, width), lambda i: (i, 0))
    out_shape = [jax.ShapeDtypeStruct((n, width), BF16)]
    out_specs = [row_spec]
    if emit_vn:
        out_shape.append(jax.ShapeDtypeStruct((n, width), F32))
        out_specs.append(row_spec)
    res = pl.pallas_call(
        functools.partial(_sgu_kernel, r=r, gd=width // groups),
        out_shape=out_shape,
        grid=(n // rb,),
        in_specs=[
            row_spec, row_spec,
            pl.BlockSpec((1, width), lambda i: (0, 0)),
            pl.BlockSpec((None, groups, r, r), lambda i: (layer_i, 0, 0, 0)),
            pl.BlockSpec((None, r, width), lambda i: (layer_i, 0, 0)),
        ],
        out_specs=out_specs,
        scratch_shapes=[pltpu.VMEM((rb, width), BF16)],
        compiler_params=_params(1),
        name="spatial_gating",
    )(u, gv, g_sgu.reshape(1, width), w_mix, bias)
    return res if emit_vn else (res[0], None)


def _gla_kernel(*refs, lc, has_state, has_carried, q_scale, layer_i):
    q_ref, k_ref, v_ref, r_ref, a_ref, wgh_ref, wgl_ref, bg_ref, gg_ref = refs[:9]
    s0_ref = refs[9] if has_state else None
    o_ref, s_ref, st_ref, lg_ref = refs[9 + has_state + has_carried:]
    tb = pl.program_id(2)
    n_rows, dk = q_ref.shape
    n_chunks = n_rows // lc

    @pl.when(tb == 0)
    def _():
        if has_state:
            st_ref[...] = s0_ref[...].T
        else:
            st_ref[...] = jnp.zeros_like(st_ref)

    a_hi, a_lo = _split_bf16(a_ref[...])
    pre = (_dot(a_hi, wgh_ref[...]) + _dot(a_lo, wgh_ref[...]) + _dot(a_hi, wgl_ref[...])
           + bg_ref[...])
    lg_ref[...] = -_softplus(-pre) * (1.0 / GLA_TAU)

    row = lax.broadcasted_iota(jnp.int32, (lc, lc), 0)
    col = lax.broadcasted_iota(jnp.int32, (lc, lc), 1)
    lower = jnp.where(col <= row, 1.0, 0.0).astype(BF16)

    chunks = [slice(c * lc, (c + 1) * lc) for c in range(n_chunks)]
    per_chunk = lambda x: [x[cs] for cs in chunks]

    def roll_in_group(x, d):
        x3 = x.reshape(n_rows // SUBLANES, SUBLANES, x.shape[1])
        return pltpu.roll(x3, d, 1).reshape(x.shape)

    lg_hi, lg_lo = _split_bf16(lg_ref[...])
    b = jnp.concatenate(
        [_dot(lower, hi) + _dot(lower, lo) for hi, lo in zip(per_chunk(lg_hi), per_chunk(lg_lo))],
        axis=0)
    q = q_ref[...].astype(F32) * q_scale
    k = k_ref[...].astype(F32)

    scores = [jnp.zeros((lc, lc), F32)] * n_chunks
    h = lc // 2
    while h >= GLA_DIRECT:
        b3 = b.reshape(n_rows // (2 * h), 2 * h, dk)
        f = jnp.exp(-jnp.abs(b3 - b3[:, h - 1:h, :])).reshape(n_rows, dk)
        qf, kf = (q * f).astype(BF16), (k * f).astype(BF16)
        mask = ((row // h) == (col // h) + 1) & ((col // h) % 2 == 0)
        scores = [sc + jnp.where(mask, _dot_nt(qc, kc), 0.0)
                  for sc, qc, kc in zip(scores, per_chunk(qf), per_chunk(kf))]
        h //= 2
    for d in range(GLA_DIRECT):
        if d == 0:
            pd = jnp.sum(q * k, axis=-1, keepdims=True)
        else:
            kd = roll_in_group(k, d)
            bd = roll_in_group(b, d)
            pd = jnp.sum(q * kd * jnp.exp(jnp.minimum(b - bd, 0.0)), axis=-1, keepdims=True)
        mask = (col == row - d) & ((row % GLA_DIRECT) >= d)
        scores = [sc + jnp.where(mask, pc, 0.0) for sc, pc in zip(scores, per_chunk(pd))]

    b_last = jnp.broadcast_to(b.reshape(n_chunks, lc, dk)[:, lc - 1:lc, :], (n_chunks, lc, dk))
    decay = [jnp.exp(bl[0:1, :]) for bl in per_chunk(b_last.reshape(n_rows, dk))]
    qd = per_chunk((q * jnp.exp(b)).astype(BF16))
    k_dec = per_chunk((k * jnp.exp(b_last.reshape(n_rows, dk) - b)).astype(BF16))
    vs = [v_ref[cs, :] for cs in chunks]
    intra = [_dot(sc.astype(BF16), vc) for sc, vc in zip(scores, vs)]
    kv = [_dot_tn(vc, kc) for vc, kc in zip(vs, k_dec)]

    st = st_ref[...]
    o_inter = _dot_nt(qd[0], st.astype(BF16))
    for c, cs in enumerate(chunks):
        o = o_inter + intra[c]
        st = st * decay[c] + kv[c]
        if c + 1 < n_chunks:
            o_inter = _dot_nt(qd[c + 1], st.astype(BF16))
        on = _rms(o, gg_ref[...])
        o_ref[cs, :] = (on * _silu(r_ref[cs, :].astype(F32))).astype(o_ref.dtype)
    st_ref[...] = st

    @pl.when(tb == pl.num_programs(2) - 1)
    def _():
        if has_carried:
            s_ref[...] = st_ref[...].T
        else:
            for l in range(s_ref.shape[0]):
                s_ref[l] = st_ref[...].T if l == layer_i else jnp.zeros(s_ref.shape[1:], F32)


def _gla(q, k, v, r, a, wg_hi, wg_lo, b_gate, g_gla, state, layer_i, n_heads, states_out,
         tb_pref=2048):
    b, l, wq = q.shape
    dk = wq // n_heads
    dv = v.shape[2] // n_heads
    ra = a.shape[2]
    lc = min(GLA_CHUNK, l)
    tb = min(tb_pref, l)
    assert l % tb == 0 and tb % lc == 0 and lc % (2 * SUBLANES) == 0
    qk_spec = pl.BlockSpec((None, tb, dk), lambda bi, h, t: (bi, t, h))
    vr_spec = pl.BlockSpec((None, tb, dv), lambda bi, h, t: (bi, t, h))
    in_specs = [
        qk_spec, qk_spec, vr_spec, vr_spec,
        pl.BlockSpec((None, tb, ra), lambda bi, h, t: (bi, t, 0)),
        pl.BlockSpec((None, ra, dk), lambda bi, h, t: (layer_i, 0, h)),
        pl.BlockSpec((None, ra, dk), lambda bi, h, t: (layer_i, 0, h)),
        pl.BlockSpec((None, 1, dk), lambda bi, h, t: (layer_i, 0, h)),
        pl.BlockSpec((None, 1, dv), lambda bi, h, t: (layer_i, 0, 0)),
    ]
    args = [q, k, v, r, a, wg_hi, wg_lo, b_gate.reshape(-1, 1, wq), g_gla.reshape(-1, 1, dv)]
    has_state = state is not None
    if has_state:
        in_specs.append(
            pl.BlockSpec((None, None, None, dk, dv), lambda bi, h, t: (layer_i, bi, h, 0, 0)))
        args.append(state)
    aliases = {}
    n_layers = b_gate.shape[0]
    if states_out is not None:
        aliases[len(args)] = 1
        in_specs.append(pl.BlockSpec(memory_space=pl.ANY))
        args.append(states_out)
        state_spec = pl.BlockSpec((None, None, None, dk, dv),
                                  lambda bi, h, t: (layer_i, bi, h, 0, 0))
    else:
        state_spec = pl.BlockSpec((n_layers, None, None, dk, dv), lambda bi, h, t: (0, bi, h, 0, 0))
    return pl.pallas_call(
        functools.partial(_gla_kernel, lc=lc, has_state=has_state,
                          has_carried=states_out is not None, q_scale=dk ** -0.5,
                          layer_i=layer_i),
        out_shape=[jax.ShapeDtypeStruct((b, l, n_heads * dv), BF16),
                   jax.ShapeDtypeStruct((n_layers, b, n_heads, dk, dv), F32)],
        grid=(b, n_heads, l // tb),
        in_specs=in_specs,
        out_specs=[vr_spec, state_spec],
        scratch_shapes=[pltpu.VMEM((dv, dk), F32), pltpu.VMEM((tb, dk), F32)],
        input_output_aliases=aliases,
        compiler_params=_params(3),
        name="gated_linear_attention",
    )(*args)


def _trunk(x3, grp, mod, weights, dims, cache_k, cache_v, state_gla):
    (g_norm, w_in_even, w_out_even, g_sgu, sgu_mix, sgu_bias, w_in_odd, w_a, wg, b_gate, g_gla,
     w_out_odd, w_ff_up, w_ff_down) = weights
    sb_heads, sb_width, sgu_width, gla_heads, gla_qk, gla_vw = dims
    n_seq, seq_len, d = x3.shape
    sample = cache_k is not None
    x = x3.reshape(n_seq * seq_len, d)
    seq = lambda t: t.reshape(n_seq, seq_len, t.shape[-1])
    depth = g_norm.shape[0]
    n_even = (depth + 1) // 2
    k_all = v_all = states = None
    gvs = []
    for layer in range(depth):
        i = layer // 2
        if layer % 2 == 0:
            sb_scale = (sb_width // sb_heads) ** -0.5
            q, k_all, v_all, u, gv = _inproj(
                x, mod, layer, g_norm[layer, 0], w_in_even, i,
                [(sb_width, BF16, sb_scale, None), (sb_width, F32, 1.0, (n_even, k_all)),
                 (sb_width, F32, 1.0, (n_even, v_all)), (sgu_width, BF16, 1.0, None),
                 (sgu_width, F32, 1.0, None)], grp)
            kv_seq = lambda t: t.reshape(n_even, n_seq, seq_len, sb_width)
            if sample:
                o_a = _sb_sample(seq(q), kv_seq(k_all), kv_seq(v_all), cache_k, cache_v, i)
            else:
                o_a = _sb_prompt(seq(q), kv_seq(k_all), kv_seq(v_all), i, sb_heads)
            o_b, vn = _sgu(u, gv, g_sgu[i], sgu_mix, sgu_bias, i, emit_vn=sample)
            x = _outproj([o_a.reshape(x.shape[0], sb_width), o_b], w_out_even, i, x, mod, layer,
                         g_norm[layer, 1], grp)
            gvs.append(vn)
        else:
            q, k, v, r, a = _inproj(
                x, mod, layer, g_norm[layer, 0], w_in_odd, i,
                [(gla_qk, BF16, 1.0, None), (gla_qk, BF16, 1.0, None), (gla_vw, BF16, 1.0, None),
                 (gla_vw, BF16, 1.0, None)], grp, aux_w=w_a)
            o, states = _gla(seq(q), seq(k), seq(v), seq(r), seq(a), wg[0], wg[1], b_gate, g_gla,
                             state_gla if sample else None, i, gla_heads, states)
            x = _outproj([o.reshape(x.shape[0], gla_vw)], w_out_odd, i, x, mod, layer,
                         g_norm[layer, 1], grp)
        x = _ff(x, mod, layer, g_norm[layer, 2], g_norm[layer, 3], w_ff_up, w_ff_down, grp)
    return x.reshape(n_seq, seq_len, d), k_all, v_all, states, gvs


def kernel(x_prompt, x_sample, cache_sb_k, cache_sb_v, state_gla, c_prompt, c_sample, w_mod, b_mod, g_norm, w_in_even, w_out_even, g_sgu, w_sgu, b_sgu, w_in_odd, w_gate_up, b_gate, g_gla, w_out_odd, w_ff_up, w_ff_down):
    batch, seq, d = x_prompt.shape
    dec_batch, dec_seq, _ = x_sample.shape
    n_even, _, past, sb_heads, sb_dh = cache_sb_k.shape
    sb_width = sb_heads * sb_dh
    _, _, gla_heads, gla_dk, gla_dv = state_gla.shape
    gla_qk, gla_vw = gla_heads * gla_dk, gla_heads * gla_dv
    _, groups, sgu_chunk, _ = w_sgu.shape
    sgu_width = g_sgu.shape[1]
    rank = w_gate_up.shape[1]
    dims = (sb_heads, sb_width, sgu_width, gla_heads, gla_qk, gla_vw)

    mod = _modulation(jnp.concatenate([c_sample, c_prompt], axis=0), w_mod, b_mod)
    mod = mod.reshape(mod.shape[0], mod.shape[1], 1, mod.shape[2])

    cast = lambda t: t.astype(BF16)
    n_main = 2 * gla_qk + 2 * gla_vw
    w_a = jnp.pad(w_in_odd[:, :, n_main:], ((0, 0), (0, 0), (0, LANES - rank)))
    w_a_hi = cast(w_a)
    w_a = (w_a_hi, cast(w_a - w_a_hi.astype(F32)))
    wg = jnp.pad(w_gate_up, ((0, 0), (0, LANES - rank), (0, 0)))
    wg_hi = cast(wg)
    wg = (wg_hi, cast(wg - wg_hi.astype(F32)))
    w_tri = w_sgu * jnp.tril(jnp.ones((sgu_chunk, sgu_chunk), F32))

    def sgu_tables(chunk_len):
        mix = cast(w_tri[:, :, :chunk_len, :chunk_len])
        bias = jnp.repeat(jnp.swapaxes(b_sgu[:, :, :chunk_len], 1, 2), sgu_width // groups, axis=2)
        return mix, bias

    common = (g_norm, cast(w_in_even), cast(w_out_even), g_sgu)
    tail = (cast(w_in_odd), w_a, wg, b_gate, g_gla, cast(w_out_odd), cast(w_ff_up), cast(w_ff_down))

    grp_p = _Group(batch, seq, dec_batch, 512)
    grp_s = _Group(dec_batch, dec_seq, 0, 512)

    y_p, ks_p, vs_p, st_p, _ = _trunk(
        x_prompt, grp_p, mod, common + sgu_tables(sgu_chunk) + tail, dims, None, None, None)
    y_s, ks_s, vs_s, st_s, gv_s = _trunk(
        x_sample, grp_s, mod, common + sgu_tables(dec_seq) + tail, dims,
        cache_sb_k, cache_sb_v, state_gla)

    heads = lambda t, b, l: t.reshape(t.shape[0], b, l, sb_heads, sb_dh)
    return (y_p, y_s,
            heads(ks_p, batch, seq), heads(vs_p, batch, seq), st_p,
            heads(ks_s, dec_batch, dec_seq), heads(vs_s, dec_batch, dec_seq), st_s,
            jnp.stack(gv_s).reshape(len(gv_s), dec_batch, dec_seq, sgu_width))
```
